```python
import jax, jax.numpy as jnp
from jax import lax
import numpy as np

D_MODEL = 1024
BATCH = 4
SEQ = 4096
DEPTH = 4

CTX_LEN = 256
GRID_W = 64
HEAD_DIM = 64
N_HEADS_A = 8
N_KV_A = 2
N_HEADS_B = 8
N_KV_B = 2
WINDOW = 128
BLOCK = 128
D_FF = -(-8 * D_MODEL // (3 * 256)) * 256
ROPE_THETA = 10000.0
NORM_EPS = 1e-6
NEG_INF = -1e30
Q_A = N_HEADS_A * HEAD_DIM
KV_A = N_KV_A * HEAD_DIM
Q_B = N_HEADS_B * HEAD_DIM
KV_B = N_KV_B * HEAD_DIM
IN_SIZES = (Q_A, KV_A, KV_A, Q_B, KV_B, KV_B, D_MODEL, D_MODEL)
IN_COLS = Q_A + 2 * KV_A + Q_B + 2 * KV_B + 2 * D_MODEL

kernel_name = "hybrid_dit_window_dense_gqa_prefix"


def rms_norm(x, g):
    xf = x.astype(jnp.float32)
    y = xf * lax.rsqrt(jnp.mean(xf * xf, axis=-1, keepdims=True) + NORM_EPS)
    return (y * g.astype(jnp.float32)).astype(x.dtype)


def modulate(h, shift, scale):
    return h * (1 + scale) + shift


def axial_rope_tables(n_tokens):
    rows = n_tokens // GRID_W
    row = jnp.repeat(jnp.arange(rows, dtype=jnp.float32), GRID_W)
    col = jnp.tile(jnp.arange(GRID_W, dtype=jnp.float32), rows)
    axis_dims = HEAD_DIM // 2
    inv = ROPE_THETA ** (-jnp.arange(0, axis_dims, 2, dtype=jnp.float32) / axis_dims)
    ang = jnp.stack([row[:, None] * inv, col[:, None] * inv], axis=1)
    return jnp.cos(ang), jnp.sin(ang)


def apply_rope(x, cos, sin):
    xr = x.astype(jnp.float32).reshape(x.shape[:-1] + (2, 2, HEAD_DIM // 4))
    x1, x2 = xr[..., 0, :], xr[..., 1, :]
    cs, sn = cos[:, None], sin[:, None]
    out = jnp.stack([x1 * cs - x2 * sn, x2 * cs + x1 * sn], axis=-2)
    return out.reshape(x.shape).astype(x.dtype)


def split_in_proj(z):
    pts = np.cumsum(IN_SIZES)[:-1].tolist()
    qa, ka, va, qb, kb, vb, ga, gb = jnp.split(z, pts, axis=-1)
    lead = z.shape[:-1]
    heads = lambda t, n: t.reshape(lead + (n, HEAD_DIM))
    return (heads(qa, N_HEADS_A), heads(ka, N_KV_A), heads(va, N_KV_A),
            heads(qb, N_HEADS_B), heads(kb, N_KV_B), heads(vb, N_KV_B), ga, gb)


def softmax_with_sink(s, sink_hg):
    sk = jnp.broadcast_to(sink_hg.astype(jnp.float32)[:, :, None, None], s.shape[:-1] + (1,))
    p = jax.nn.softmax(jnp.concatenate([s, sk], axis=-1), axis=-1)
    return p[..., :-1]


def window_attention_latent(q, k, v, kc, vc, sink):
    B, S = q.shape[:2]
    C = kc.shape[1]
    nb = S // BLOCK
    G = N_HEADS_A // N_KV_A
    scale = HEAD_DIM ** -0.5
    qb = q.reshape(B, nb, BLOCK, N_KV_A, G, HEAD_DIM)
    pad = ((0, 0), (BLOCK, BLOCK), (0, 0), (0, 0))
    kp = jnp.pad(k, pad).reshape(B, nb + 2, BLOCK, N_KV_A, HEAD_DIM)
    vp = jnp.pad(v, pad).reshape(B, nb + 2, BLOCK, N_KV_A, HEAD_DIM)
    kw = jnp.concatenate([kp[:, :-2], kp[:, 1:-1], kp[:, 2:]], axis=2)
    vw = jnp.concatenate([vp[:, :-2], vp[:, 1:-1], vp[:, 2:]], axis=2)
    s_win = jnp.einsum('bnqhgd,bnkhd->bnhgqk', qb, kw).astype(jnp.float32) * scale
    qpos = jnp.arange(nb)[:, None] * BLOCK + jnp.arange(BLOCK)[None, :]
    kpos = jnp.arange(nb)[:, None] * BLOCK - BLOCK + jnp.arange(3 * BLOCK)[None, :]
    rel = kpos[:, None, :] - qpos[:, :, None]
    valid = (jnp.abs(rel) <= WINDOW) & (kpos[:, None, :] >= 0) & (kpos[:, None, :] < S)
    s_win = jnp.where(valid[None, :, None, None], s_win, NEG_INF)
    s_ctx = jnp.einsum('bnqhgd,bchd->bnhgqc', qb, kc).astype(jnp.float32) * scale
    p = softmax_with_sink(jnp.concatenate([s_win, s_ctx], axis=-1), sink.reshape(N_KV_A, G)).astype(v.dtype)
    out = (jnp.einsum('bnhgqk,bnkhd->bnqhgd', p[..., :3 * BLOCK], vw)
           + jnp.einsum('bnhgqc,bchd->bnqhgd', p[..., 3 * BLOCK:], vc))
    return out.reshape(B, S, Q_A)


def full_attention_latent(q, k, v, kc, vc):
    B, S = q.shape[:2]
    nb = S // BLOCK
    G = N_HEADS_B // N_KV_B
    scale = HEAD_DIM ** -0.5
    kall = jnp.concatenate([k, kc], axis=1)
    vall = jnp.concatenate([v, vc], axis=1)
    qb = q.reshape(B, nb, BLOCK, N_KV_B, G, HEAD_DIM).transpose(1, 0, 2, 3, 4, 5)

    def one_block(qblk):
        s = jnp.einsum('bqhgd,bkhd->bhgqk', qblk, kall).astype(jnp.float32) * scale
        p = jax.nn.softmax(s, axis=-1).astype(vall.dtype)
        return jnp.einsum('bhgqk,bkhd->bqhgd', p, vall)

    out = lax.map(one_block, qb)
    return out.transpose(1, 0, 2, 3, 4, 5).reshape(B, S, Q_B)


def context_attention(q, k, v, n_kv, sink):
    B, C, H, _ = q.shape
    G = H // n_kv
    qg = q.reshape(B, C, n_kv, G, HEAD_DIM)
    s = jnp.einsum('bqhgd,bkhd->bhgqk', qg, k).astype(jnp.float32) * (HEAD_DIM ** -0.5)
    if sink is None:
        p = jax.nn.softmax(s, axis=-1)
    else:
        p = softmax_with_sink(s, sink.reshape(n_kv, G))
    out = jnp.einsum('bhgqk,bkhd->bqhgd', p.astype(v.dtype), v)
    return out.reshape(B, C, H * HEAD_DIM)


def merge_branches(ya, yb, ga, gb, w_pa, w_pb, w_o):
    m = jax.nn.sigmoid(ga) * (ya @ w_pa) + jax.nn.sigmoid(gb) * (yb @ w_pb)
    return m @ w_o


def swiglu(h, w_gate, w_up, w_down):
    return (jax.nn.silu(h @ w_gate) * (h @ w_up)) @ w_down


def setup_inputs(seed: int = 0) -> dict:
    key = jax.random.key(seed)
    ks = jax.random.split(key, 20)
    f32 = jnp.float32
    nrm = lambda k, shape, s: jax.random.normal(k, shape, f32) * s
    return {
        "x": nrm(ks[0], (BATCH, SEQ, D_MODEL), 1.0),
        "c": nrm(ks[1], (BATCH, D_MODEL), 1.0),
        "ctx": nrm(ks[2], (BATCH, CTX_LEN, D_MODEL), 1.0),
        "c_ctx": nrm(ks[3], (D_MODEL,), 1.0),
        "w_ada": nrm(ks[4], (DEPTH, D_MODEL, 6 * D_MODEL), 0.5 * D_MODEL ** -0.5),
        "b_ada": nrm(ks[5], (DEPTH, 6 * D_MODEL), 0.01),
        "norm1_g": 1.0 + nrm(ks[6], (DEPTH, D_MODEL), 0.02),
        "norm2_g": 1.0 + nrm(ks[7], (DEPTH, D_MODEL), 0.02),
        "w_in": nrm(ks[8], (DEPTH, D_MODEL, IN_COLS), D_MODEL ** -0.5),
        "q_norm_g": 1.0 + nrm(ks[9], (DEPTH, HEAD_DIM), 0.02),
        "k_norm_g": 1.0 + nrm(ks[10], (DEPTH, HEAD_DIM), 0.02),
        "sink_a": nrm(ks[11], (DEPTH, N_HEADS_A), 0.5),
        "w_proj_a": nrm(ks[12], (DEPTH, Q_A, D_MODEL), Q_A ** -0.5),
        "w_proj_b": nrm(ks[13], (DEPTH, Q_B, D_MODEL), Q_B ** -0.5),
        "w_out": nrm(ks[14], (DEPTH, D_MODEL, D_MODEL), D_MODEL ** -0.5),
        "w_ffn_gate": nrm(ks[15], (DEPTH, D_MODEL, D_FF), D_MODEL ** -0.5),
        "w_ffn_up": nrm(ks[16], (DEPTH, D_MODEL, D_FF), D_MODEL ** -0.5),
        "w_ffn_down": nrm(ks[17], (DEPTH, D_FF, D_MODEL), D_FF ** -0.5),
        "final_norm_g": 1.0 + nrm(ks[18], (D_MODEL,), 0.02),
    }


def reference(x, c, ctx, c_ctx, w_ada, b_ada, norm1_g, norm2_g, w_in, q_norm_g, k_norm_g, sink_a,
              w_proj_a, w_proj_b, w_out, w_ffn_gate, w_ffn_up, w_ffn_down, final_norm_g):
    cos, sin = axial_rope_tables(x.shape[1])
    silu_c = jax.nn.silu(c)
    silu_cc = jax.nn.silu(c_ctx)
    xc = ctx
    for l in range(DEPTH):
        last = l == DEPTH - 1
        mod = silu_c @ w_ada[l] + b_ada[l]
        mod_c = silu_cc @ w_ada[l] + b_ada[l]
        sh1, sc1, gt1, sh2, sc2, gt2 = jnp.split(mod[:, None, :], 6, axis=-1)
        csh1, csc1, cgt1, csh2, csc2, cgt2 = jnp.split(mod_c, 6, axis=-1)

        h = modulate(rms_norm(x, norm1_g[l]), sh1, sc1)
        hc = modulate(rms_norm(xc, norm1_g[l]), csh1, csc1)
        qa, ka, va, qb, kb, vb, ga, gb = split_in_proj(h @ w_in[l])
        qac, kac, vac, qbc, kbc, vbc, gac, gbc = split_in_proj(hc @ w_in[l])
        qb = rms_norm(qb, q_norm_g[l])
        kb = rms_norm(kb, k_norm_g[l])
        kbc = rms_norm(kbc, k_norm_g[l])
        ya = window_attention_latent(apply_rope(qa, cos, sin), apply_rope(ka, cos, sin), va, kac, vac, sink_a[l])
        yb = full_attention_latent(apply_rope(qb, cos, sin), apply_rope(kb, cos, sin), vb, kbc, vbc)
        x = x + gt1 * merge_branches(ya, yb, ga, gb, w_proj_a[l], w_proj_b[l], w_out[l])

        h2 = modulate(rms_norm(x, norm2_g[l]), sh2, sc2)
        x = x + gt2 * swiglu(h2, w_ffn_gate[l], w_ffn_up[l], w_ffn_down[l])

        if not last:
            qbc = rms_norm(qbc, q_norm_g[l])
            yac = context_attention(qac, kac, vac, N_KV_A, sink_a[l])
            ybc = context_attention(qbc, kbc, vbc, N_KV_B, None)
            xc = xc + cgt1 * merge_branches(yac, ybc, gac, gbc, w_proj_a[l], w_proj_b[l], w_out[l])
            h2c = modulate(rms_norm(xc, norm2_g[l]), csh2, csc2)
            xc = xc + cgt2 * swiglu(h2c, w_ffn_gate[l], w_ffn_up[l], w_ffn_down[l])
    return rms_norm(x, final_norm_g)
```

```python
import functools

import jax
import jax.numpy as jnp
import numpy as np
from jax import lax
from jax.experimental import pallas as pl
from jax.experimental.pallas import tpu as pltpu

D_MODEL = 1024
BATCH = 4
SEQ = 4096
DEPTH = 4
CTX_LEN = 256
GRID_W = 64
HEAD_DIM = 64
N_HEADS = 8
N_KV = 2
WINDOW = 128
BLOCK = 128
D_FF = 2816
ROPE_THETA = 10000.0
NORM_EPS = 1e-6
NEG_INF = -1e30
Q_W = N_HEADS * HEAD_DIM
KV_W = N_KV * HEAD_DIM
IN_COLS = 2 * (Q_W + 2 * KV_W) + 2 * D_MODEL
LANES = 128
PAD_W = 2 * N_KV * LANES

N_LAT = BATCH * SEQ
N_CTX = BATCH * CTX_LEN
N_TOK = N_LAT + N_CTX
MOD_ROWS = 8
CTX_MOD_ROW = BATCH

TM = 512
LAT_TILES = N_LAT // TM
ALL_TILES = N_TOK // TM
TILES_PER_SEQ = SEQ // TM
TQ_DENSE = 256
TK_DENSE = 1024
VMEM_LIMIT = 56 * 1024 * 1024

_F32 = jnp.float32
_BF16 = jnp.bfloat16
_NT = (((1,), (1,)), ((), ()))


def _params(sem):
    return pltpu.CompilerParams(dimension_semantics=sem, vmem_limit_bytes=VMEM_LIMIT)


def _resident(shape):
    nd = len(shape)
    return pl.BlockSpec(shape, lambda *_: (0,) * nd, pipeline_mode=pl.Buffered(1))


def _mod_row_of_tile(i):
    return jnp.minimum(i // TILES_PER_SEQ, CTX_MOD_ROW)


def _mod_spec():
    return pl.BlockSpec((None, 1, 6 * D_MODEL), lambda i: (_mod_row_of_tile(i), 0, 0))


def _mod_kernel(c_ref, w_ref, b_ref, o_ref):
    c = c_ref[...]
    a = c / (1.0 + jnp.exp(-c))
    o_ref[...] = jnp.dot(a, w_ref[...], preferred_element_type=_F32) + b_ref[...]


def _modulation(cc, w_ada, b_ada):
    tn = 1536
    return pl.pallas_call(
        _mod_kernel,
        grid=(DEPTH, 6 * D_MODEL // tn),
        in_specs=[
            pl.BlockSpec((MOD_ROWS, D_MODEL), lambda l, j: (0, 0)),
            pl.BlockSpec((None, D_MODEL, tn), lambda l, j: (l, 0, j)),
            pl.BlockSpec((None, 1, tn), lambda l, j: (l, 0, j)),
        ],
        out_specs=pl.BlockSpec((None, MOD_ROWS, tn), lambda l, j: (l, 0, j)),
        out_shape=jax.ShapeDtypeStruct((DEPTH, MOD_ROWS, 6 * D_MODEL), _F32),
        compiler_params=_params(("arbitrary", "arbitrary")),
        name="adaln_mod",
    )(cc, w_ada, b_ada.reshape(DEPTH, 1, 6 * D_MODEL))


def _norm_modulate(x, g, shift, scale):
    ms = jnp.mean(x * x, axis=-1, keepdims=True)
    return (x * lax.rsqrt(ms + NORM_EPS)) * (g * (1.0 + scale)) + shift


def _rope(z, cos, sin_up, sin_dn):
    return z * cos + pltpu.roll(z, 16, 1) * sin_up + pltpu.roll(z, LANES - 16, 1) * sin_dn


def _head_rms(z, gain, lo):
    z2 = z * z
    s_all = jnp.sum(z2, axis=-1, keepdims=True)
    s_lo = jnp.sum(jnp.where(lo, z2, 0.0), axis=-1, keepdims=True)
    ms = jnp.where(lo, s_lo, s_all - s_lo) * (1.0 / HEAD_DIM)
    return (z * lax.rsqrt(ms + NORM_EPS)) * gain


def _pad_halves(z, lo):
    sw = pltpu.roll(z, HEAD_DIM, 1)
    zero = jnp.zeros_like(z)
    parts = [jnp.where(lo, z, zero), jnp.where(lo, zero, sw),
             jnp.where(lo, sw, zero), jnp.where(lo, zero, z)]
    return jnp.concatenate(parts, axis=1).astype(_BF16)


def _sigmoid(z):
    return 1.0 / (1.0 + jnp.exp(-z))


def _in_proj_kernel(x_ref, mod_ref, g_ref, w_ref, rope_ref, qg_ref, kg_ref,
                    qa_ref, ka_ref, va_ref, qb_ref, kb_ref, vb_ref, ga_ref, gb_ref):
    x = x_ref[...]
    shift = mod_ref[:, 0:D_MODEL]
    scale = mod_ref[:, D_MODEL:2 * D_MODEL]
    h = _norm_modulate(x, g_ref[...], shift, scale).astype(_BF16)
    cos = rope_ref[:, 0:LANES]
    sin_up = rope_ref[:, LANES:2 * LANES]
    sin_dn = rope_ref[:, 2 * LANES:3 * LANES]
    lo = lax.broadcasted_iota(jnp.int32, (1, LANES), 1) < HEAD_DIM
    qscale = HEAD_DIM ** -0.5

    def proj(c0, width):
        return jnp.dot(h, w_ref[:, c0:c0 + width], preferred_element_type=_F32)

    c = 0
    z = proj(c, Q_W)
    for j in range(Q_W // LANES):
        zj = z[:, j * LANES:(j + 1) * LANES]
        qa_ref[:, j * LANES:(j + 1) * LANES] = (_rope(zj, cos, sin_up, sin_dn) * qscale).astype(_BF16)
    c += Q_W
    ka_ref[...] = _pad_halves(_rope(proj(c, KV_W), cos, sin_up, sin_dn), lo)
    c += KV_W
    va_ref[...] = _pad_halves(proj(c, KV_W), lo)
    c += KV_W
    z = proj(c, Q_W)
    qg = qg_ref[...]
    for j in range(Q_W // LANES):
        zj = _head_rms(z[:, j * LANES:(j + 1) * LANES], qg, lo)
        qb_ref[:, j * LANES:(j + 1) * LANES] = (_rope(zj, cos, sin_up, sin_dn) * qscale).astype(_BF16)
    c += Q_W
    zk = _head_rms(proj(c, KV_W), kg_ref[...], lo)
    kb_ref[...] = _pad_halves(_rope(zk, cos, sin_up, sin_dn), lo)
    c += KV_W
    vb_ref[...] = _pad_halves(proj(c, KV_W), lo)
    c += KV_W
    ga_ref[...] = _sigmoid(proj(c, D_MODEL)).astype(_BF16)
    c += D_MODEL
    gb_ref[...] = _sigmoid(proj(c, D_MODEL)).astype(_BF16)


def _in_proj(x_all, mod_l, g1, w_in, rope_tab, qg, kg):
    tile = lambda w: pl.BlockSpec((TM, w), lambda i: (i, 0))
    rope_spec = pl.BlockSpec(
        (TM, 3 * LANES), lambda i: (jnp.where(i < LAT_TILES, i % TILES_PER_SEQ, TILES_PER_SEQ), 0))
    widths = [Q_W, PAD_W, PAD_W, Q_W, PAD_W, PAD_W, D_MODEL, D_MODEL]
    return pl.pallas_call(
        _in_proj_kernel,
        grid=(ALL_TILES,),
        in_specs=[tile(D_MODEL), _mod_spec(), _resident((1, D_MODEL)),
                  _resident((D_MODEL, IN_COLS)), rope_spec,
                  _resident((1, LANES)), _resident((1, LANES))],
        out_specs=[tile(w) for w in widths],
        out_shape=[jax.ShapeDtypeStruct((N_TOK, w), _BF16) for w in widths],
        compiler_params=_params(("arbitrary",)),
        name="in_proj",
    )(x_all, mod_l, g1, w_in, rope_tab, qg, kg)


def _stack_pairs(q, g):
    return jnp.concatenate([q[:, (2 * g) * LANES:(2 * g + 1) * LANES],
                            q[:, (2 * g + 1) * LANES:(2 * g + 2) * LANES]], axis=0)


def _softmax_pv(scores, values, sink_col):
    m = sink_col
    for s in scores:
        sm = jnp.max(s, axis=-1, keepdims=True)
        m = sm if m is None else jnp.maximum(m, sm)
    den = None if sink_col is None else jnp.exp(sink_col - m)
    out = None
    for s, v in zip(scores, values):
        p = jnp.exp(s - m)
        ps = jnp.sum(p, axis=-1, keepdims=True)
        den = ps if den is None else den + ps
        o = jnp.dot(p.astype(_BF16), v, preferred_element_type=_F32)
        out = o if out is None else out + o
    return out / den


def _window_kernel(sink_ref, q_ref, k_ref, v_ref, kc_ref, vc_ref, o_ref):
    n = pl.program_id(1)
    nb = SEQ // BLOCK
    start = pl.multiple_of(jnp.clip(n - 1, 0, nb - 3) * BLOCK, BLOCK)
    rows = 2 * BLOCK
    r = lax.broadcasted_iota(jnp.int32, (rows, 3 * BLOCK), 0)
    col = lax.broadcasted_iota(jnp.int32, (rows, 3 * BLOCK), 1)
    rel = (start + col) - (n * BLOCK + (r & (BLOCK - 1)))
    valid = jnp.abs(rel) <= WINDOW
    top = lax.broadcasted_iota(jnp.int32, (rows, 1), 0) < BLOCK
    q = q_ref[...]
    for g in range(N_KV):
        qs = _stack_pairs(q, g)
        acc = None
        for e in range(2):
            cs = slice((2 * g + e) * LANES, (2 * g + e + 1) * LANES)
            kw = k_ref[pl.ds(start, 3 * BLOCK), cs]
            vw = v_ref[pl.ds(start, 3 * BLOCK), cs]
            s_w = lax.dot_general(qs, kw, _NT, preferred_element_type=_F32)
            s_w = jnp.where(valid, s_w, NEG_INF)
            s_c = lax.dot_general(qs, kc_ref[:, cs], _NT, preferred_element_type=_F32)
            sink_col = jnp.where(top, sink_ref[4 * g + e], sink_ref[4 * g + 2 + e])
            o = _softmax_pv([s_w, s_c], [vw, vc_ref[:, cs]], sink_col)
            acc = o if acc is None else acc + o
        o_ref[:, (2 * g) * LANES:(2 * g + 1) * LANES] = acc[:BLOCK].astype(_BF16)
        o_ref[:, (2 * g + 1) * LANES:(2 * g + 2) * LANES] = acc[BLOCK:].astype(_BF16)


def _window_attention(sink, qa, ka, va):
    nb = SEQ // BLOCK
    lat_kv = pl.BlockSpec((SEQ, PAD_W), lambda b, n: (b, 0))
    ctx_kv = pl.BlockSpec((CTX_LEN, PAD_W), lambda b, n: (N_LAT // CTX_LEN + b, 0))
    q_spec = pl.BlockSpec((BLOCK, Q_W), lambda b, n: (b * nb + n, 0))
    return pl.pallas_call(
        _window_kernel,
        grid=(BATCH, nb),
        in_specs=[pl.BlockSpec(memory_space=pltpu.SMEM), q_spec, lat_kv, lat_kv, ctx_kv, ctx_kv],
        out_specs=q_spec,
        out_shape=jax.ShapeDtypeStruct((N_LAT, Q_W), _BF16),
        compiler_params=_params(("arbitrary", "arbitrary")),
        name="window_attn",
    )(sink, qa, ka, va, ka, va)


def _dense_kernel(q_ref, k_ref, v_ref, kc_ref, vc_ref, o_ref, m_ref, l_ref, acc_ref):
    rows = 2 * TQ_DENSE
    q = q_ref[...]
    qs = jnp.concatenate([q[:, 0:LANES], q[:, LANES:2 * LANES]], axis=0)
    m_ref[...] = jnp.full(m_ref.shape, -jnp.inf, _F32)
    l_ref[...] = jnp.zeros(l_ref.shape, _F32)
    acc_ref[...] = jnp.zeros(acc_ref.shape, _F32)

    def update(e, k, v):
        s = lax.dot_general(qs, k, _NT, preferred_element_type=_F32)
        m_prev = m_ref[e]
        m_new = jnp.maximum(m_prev, jnp.max(s, axis=-1, keepdims=True))
        alpha = jnp.exp(m_prev - m_new)
        p = jnp.exp(s - m_new[:, 0:1])
        l_ref[e] = alpha * l_ref[e] + jnp.sum(p, axis=-1, keepdims=True)
        acc_ref[e] = alpha * acc_ref[e] + jnp.dot(p.astype(_BF16), v, preferred_element_type=_F32)
        m_ref[e] = m_new

    def chunk(j, carry):
        start = pl.multiple_of(j * TK_DENSE, TK_DENSE)
        for e in range(2):
            cs = slice(e * LANES, (e + 1) * LANES)
            update(e, k_ref[pl.ds(start, TK_DENSE), cs], v_ref[pl.ds(start, TK_DENSE), cs])
        return carry

    lax.fori_loop(0, SEQ // TK_DENSE, chunk, 0)
    for e in range(2):
        cs = slice(e * LANES, (e + 1) * LANES)
        update(e, kc_ref[:, cs], vc_ref[:, cs])
    out = acc_ref[0] / l_ref[0] + acc_ref[1] / l_ref[1]
    o_ref[:, 0:LANES] = out[:TQ_DENSE].astype(_BF16)
    o_ref[:, LANES:2 * LANES] = out[TQ_DENSE:].astype(_BF16)


def _dense_attention(qb, kb, vb):
    nq = SEQ // TQ_DENSE
    gw = 2 * LANES
    lat_kv = pl.BlockSpec((SEQ, gw), lambda b, g, i: (b, g))
    ctx_kv = pl.BlockSpec((CTX_LEN, gw), lambda b, g, i: (N_LAT // CTX_LEN + b, g))
    q_spec = pl.BlockSpec((TQ_DENSE, gw), lambda b, g, i: (b * nq + i, g))
    stat = pltpu.VMEM((2, 2 * TQ_DENSE, LANES), _F32)
    return pl.pallas_call(
        _dense_kernel,
        grid=(BATCH, N_KV, nq),
        in_specs=[q_spec, lat_kv, lat_kv, ctx_kv, ctx_kv],
        out_specs=q_spec,
        out_shape=jax.ShapeDtypeStruct((N_LAT, Q_W), _BF16),
        scratch_shapes=[stat, stat, stat],
        compiler_params=_params(("arbitrary", "arbitrary", "arbitrary")),
        name="dense_attn",
    )(qb, kb, vb, kb, vb)


def _ctx_kernel(sink_ref, qa_ref, ka_ref, va_ref, qb_ref, kb_ref, vb_ref, oa_ref, ob_ref):
    top = lax.broadcasted_iota(jnp.int32, (2 * CTX_LEN, 1), 0) < CTX_LEN
    for q_ref, k_ref, v_ref, o_ref, use_sink in ((qa_ref, ka_ref, va_ref, oa_ref, True),
                                                 (qb_ref, kb_ref, vb_ref, ob_ref, False)):
        q = q_ref[...]
        for g in range(N_KV):
            qs = _stack_pairs(q, g)
            acc = None
            for e in range(2):
                cs = slice((2 * g + e) * LANES, (2 * g + e + 1) * LANES)
                s = lax.dot_general(qs, k_ref[:, cs], _NT, preferred_element_type=_F32)
                sink_col = None
                if use_sink:
                    sink_col = jnp.where(top, sink_ref[4 * g + e], sink_ref[4 * g + 2 + e])
                o = _softmax_pv([s], [v_ref[:, cs]], sink_col)
                acc = o if acc is None else acc + o
            o_ref[:, (2 * g) * LANES:(2 * g + 1) * LANES] = acc[:CTX_LEN].astype(_BF16)
            o_ref[:, (2 * g + 1) * LANES:(2 * g + 2) * LANES] = acc[CTX_LEN:].astype(_BF16)


def _ctx_attention(sink, qa, ka, va, qb, kb, vb):
    row = lambda w: pl.BlockSpec((CTX_LEN, w), lambda b: (N_LAT // CTX_LEN + b, 0))
    out = pl.BlockSpec((CTX_LEN, Q_W), lambda b: (b, 0))
    return pl.pallas_call(
        _ctx_kernel,
        grid=(BATCH,),
        in_specs=[pl.BlockSpec(memory_space=pltpu.SMEM),
                  row(Q_W), row(PAD_W), row(PAD_W), row(Q_W), row(PAD_W), row(PAD_W)],
        out_specs=[out, out],
        out_shape=[jax.ShapeDtypeStruct((N_CTX, Q_W), _BF16)] * 2,
        compiler_params=_params(("arbitrary",)),
        name="ctx_attn",
    )(sink, qa, ka, va, qb, kb, vb)


def _merge_kernel(x_ref, mod_ref, ya_ref, yb_ref, ga_ref, gb_ref, wa_ref, wb_ref, wo_ref, o_ref):
    pa = jnp.dot(ya_ref[...], wa_ref[...], preferred_element_type=_F32)
    pb = jnp.dot(yb_ref[...], wb_ref[...], preferred_element_type=_F32)
    m = ga_ref[...].astype(_F32) * pa + gb_ref[...].astype(_F32) * pb
    y = jnp.dot(m.astype(_BF16), wo_ref[...], preferred_element_type=_F32)
    gate = mod_ref[:, 2 * D_MODEL:3 * D_MODEL]
    o_ref[...] = x_ref[...] + gate * y


def _merge(x_all, mod_l, ya, yb, ga, gb, w_pa, w_pb, w_o, n_tiles):
    tile = lambda w: pl.BlockSpec((TM, w), lambda i: (i, 0))
    return pl.pallas_call(
        _merge_kernel,
        grid=(n_tiles,),
        in_specs=[tile(D_MODEL), _mod_spec(), tile(Q_W), tile(Q_W), tile(D_MODEL), tile(D_MODEL),
                  _resident((Q_W, D_MODEL)), _resident((Q_W, D_MODEL)),
                  _resident((D_MODEL, D_MODEL))],
        out_specs=tile(D_MODEL),
        out_shape=jax.ShapeDtypeStruct((n_tiles * TM, D_MODEL), _F32),
        compiler_params=_params(("arbitrary",)),
        name="merge",
    )(x_all, mod_l, ya, yb, ga, gb, w_pa, w_pb, w_o)


def _ffn_kernel(x_ref, mod_ref, g_ref, wg_ref, wu_ref, wd_ref, gf_ref, o_ref, *, final_norm):
    x = x_ref[...]
    shift = mod_ref[:, 3 * D_MODEL:4 * D_MODEL]
    scale = mod_ref[:, 4 * D_MODEL:5 * D_MODEL]
    gate = mod_ref[:, 5 * D_MODEL:6 * D_MODEL]
    h = _norm_modulate(x, g_ref[...], shift, scale).astype(_BF16)
    a = jnp.dot(h, wg_ref[...], preferred_element_type=_F32)
    u = jnp.dot(h, wu_ref[...], preferred_element_type=_F32)
    act = ((a / (1.0 + jnp.exp(-a))) * u).astype(_BF16)
    y = x + gate * jnp.dot(act, wd_ref[...], preferred_element_type=_F32)
    if final_norm:
        ms = jnp.mean(y * y, axis=-1, keepdims=True)
        y = (y * lax.rsqrt(ms + NORM_EPS)) * gf_ref[...]
    o_ref[...] = y


def _ffn(x_all, mod_l, g2, w_gate, w_up, w_down, gf, n_tiles, final_norm):
    tile = pl.BlockSpec((TM, D_MODEL), lambda i: (i, 0))
    return pl.pallas_call(
        functools.partial(_ffn_kernel, final_norm=final_norm),
        grid=(n_tiles,),
        in_specs=[tile, _mod_spec(), _resident((1, D_MODEL)),
                  _resident((D_MODEL, D_FF)), _resident((D_MODEL, D_FF)),
                  _resident((D_FF, D_MODEL)), _resident((1, D_MODEL))],
        out_specs=tile,
        out_shape=jax.ShapeDtypeStruct((n_tiles * TM, D_MODEL), _F32),
        compiler_params=_params(("arbitrary",)),
        name="ffn",
    )(x_all, mod_l, g2, w_gate, w_up, w_down, gf)


def _rope_table():
    rows = SEQ // GRID_W
    row = jnp.repeat(jnp.arange(rows, dtype=_F32), GRID_W)
    col = jnp.tile(jnp.arange(GRID_W, dtype=_F32), rows)
    axis_dims = HEAD_DIM // 2
    inv = ROPE_THETA ** (-jnp.arange(0, axis_dims, 2, dtype=_F32) / axis_dims)
    ang = jnp.stack([row[:, None] * inv, col[:, None] * inv], axis=1)
    cos, sin = jnp.cos(ang), jnp.sin(ang)
    lane = np.arange(LANES)
    axis = (lane % HEAD_DIM) // axis_dims
    upper = ((lane % axis_dims) // (axis_dims // 2)) == 1
    j = lane % (axis_dims // 2)
    cos_l = cos[:, axis, j]
    sin_l = sin[:, axis, j]
    sin_up = jnp.where(upper[None, :], sin_l, 0.0)
    sin_dn = jnp.where(upper[None, :], 0.0, -sin_l)
    lat = jnp.concatenate([cos_l, sin_up, sin_dn], axis=1)
    ident = jnp.concatenate([jnp.ones((TM, LANES), _F32), jnp.zeros((TM, 2 * LANES), _F32)], axis=1)
    return jnp.concatenate([lat, ident], axis=0)


def kernel(x, c, ctx, c_ctx, w_ada, b_ada, norm1_g, norm2_g, w_in, q_norm_g, k_norm_g, sink_a,
           w_proj_a, w_proj_b, w_out, w_ffn_gate, w_ffn_up, w_ffn_down, final_norm_g):
    assert x.shape == (BATCH, SEQ, D_MODEL) and ctx.shape == (BATCH, CTX_LEN, D_MODEL)
    x_all = jnp.concatenate([x.reshape(N_LAT, D_MODEL), ctx.reshape(N_CTX, D_MODEL)], axis=0)
    cc = jnp.concatenate([c, c_ctx[None, :], jnp.zeros((MOD_ROWS - BATCH - 1, D_MODEL), _F32)], axis=0)
    mod = _modulation(cc, w_ada, b_ada).reshape(DEPTH, MOD_ROWS, 1, 6 * D_MODEL)
    rope_tab = _rope_table()
    gf = final_norm_g.reshape(1, D_MODEL)
    out = None
    for l in range(DEPTH):
        last = l == DEPTH - 1
        bf = lambda w: w[l].astype(_BF16)
        qa, ka, va, qb, kb, vb, ga, gb = _in_proj(
            x_all, mod[l], norm1_g[l].reshape(1, D_MODEL), bf(w_in), rope_tab,
            jnp.tile(q_norm_g[l], 2).reshape(1, LANES), jnp.tile(k_norm_g[l], 2).reshape(1, LANES))
        ya = _window_attention(sink_a[l], qa, ka, va)
        yb = _dense_attention(qb, kb, vb)
        n_tiles = LAT_TILES if last else ALL_TILES
        if not last:
            yac, ybc = _ctx_attention(sink_a[l], qa, ka, va, qb, kb, vb)
            ya = jnp.concatenate([ya, yac], axis=0)
            yb = jnp.concatenate([yb, ybc], axis=0)
        x_mid = _merge(x_all, mod[l], ya, yb, ga, gb, bf(w_proj_a), bf(w_proj_b), bf(w_out), n_tiles)
        x_new = _ffn(x_mid, mod[l], norm2_g[l].reshape(1, D_MODEL), bf(w_ffn_gate), bf(w_ffn_up),
                     bf(w_ffn_down), gf, n_tiles, last)
        if last:
            out = x_new
        else:
            x_all = x_new
    return out.reshape(BATCH, SEQ, D_MODEL)
```

```python
import functools

import jax
import jax.numpy as jnp
import numpy as np
from jax import lax
from jax.experimental import pallas as pl
from jax.experimental.pallas import tpu as pltpu

D_MODEL = 1024
BATCH = 4
SEQ = 4096
DEPTH = 4
CTX_LEN = 256
GRID_W = 64
HEAD_DIM = 64
N_HEADS = 8
N_KV = 2
WINDOW = 128
BLOCK = 128
D_FF = 2816
ROPE_THETA = 10000.0
NORM_EPS = 1e-6
NEG_INF = -1e30
Q_W = N_HEADS * HEAD_DIM
KV_W = N_KV * HEAD_DIM
IN_COLS = 2 * (Q_W + 2 * KV_W) + 2 * D_MODEL
LANES = 128
PAD_W = 2 * N_KV * LANES

N_LAT = BATCH * SEQ
N_CTX = BATCH * CTX_LEN
N_TOK = N_LAT + N_CTX
MOD_ROWS = 8
CTX_MOD_ROW = BATCH

TM = 512
LAT_TILES = N_LAT // TM
ALL_TILES = N_TOK // TM
TILES_PER_SEQ = SEQ // TM
TQ_DENSE = 512
TQ_DENSE_UNBOUNDED = 256
TK_DENSE = 1024
LOG2E = 1.4426950408889634
TK_SUB = 256
SCORE_BOUND = 60.0
VMEM_LIMIT = 56 * 1024 * 1024

_F32 = jnp.float32
_BF16 = jnp.bfloat16
_NT = (((1,), (1,)), ((), ()))


def _params(sem):
    return pltpu.CompilerParams(dimension_semantics=sem, vmem_limit_bytes=VMEM_LIMIT)


def _resident(shape):
    nd = len(shape)
    return pl.BlockSpec(shape, lambda *_: (0,) * nd, pipeline_mode=pl.Buffered(1))


def _mod_row_of_tile(i):
    return jnp.minimum(i // TILES_PER_SEQ, CTX_MOD_ROW)


def _mod_spec():
    return pl.BlockSpec((None, 1, 6 * D_MODEL), lambda i: (_mod_row_of_tile(i), 0, 0))


def _split_specs(width, ctx_tile0):
    lat = pl.BlockSpec((TM, width), lambda i: (jnp.minimum(i, LAT_TILES - 1), 0))
    ctx = pl.BlockSpec((TM, width), lambda i: (jnp.maximum(i - LAT_TILES, 0) + ctx_tile0, 0))
    return lat, ctx


def _pick_stream(lat_ref, ctx_ref):
    return jnp.where(pl.program_id(0) < LAT_TILES, lat_ref[...], ctx_ref[...])


def _mod_kernel(c_ref, w_ref, b_ref, o_ref):
    c = c_ref[...]
    a = c / (1.0 + jnp.exp(-c))
    o_ref[...] = jnp.dot(a, w_ref[...], preferred_element_type=_F32) + b_ref[...]


def _modulation(cc, w_ada, b_ada):
    tn = 1536
    return pl.pallas_call(
        _mod_kernel,
        grid=(DEPTH, 6 * D_MODEL // tn),
        in_specs=[
            pl.BlockSpec((MOD_ROWS, D_MODEL), lambda l, j: (0, 0)),
            pl.BlockSpec((None, D_MODEL, tn), lambda l, j: (l, 0, j)),
            pl.BlockSpec((None, 1, tn), lambda l, j: (l, 0, j)),
        ],
        out_specs=pl.BlockSpec((None, MOD_ROWS, tn), lambda l, j: (l, 0, j)),
        out_shape=jax.ShapeDtypeStruct((DEPTH, MOD_ROWS, 6 * D_MODEL), _F32),
        compiler_params=_params(("arbitrary", "arbitrary")),
        name="adaln_mod",
    )(cc, w_ada, b_ada.reshape(DEPTH, 1, 6 * D_MODEL))


def _norm_modulate(x, g, shift, scale):
    ms = jnp.mean(x * x, axis=-1, keepdims=True)
    return (x * lax.rsqrt(ms + NORM_EPS)) * (g * (1.0 + scale)) + shift


def _rope(z, cos, sin_up, sin_dn):
    return z * cos + pltpu.roll(z, 16, 1) * sin_up + pltpu.roll(z, LANES - 16, 1) * sin_dn


def _head_rms(z, gain, lo):
    z2 = z * z
    s_all = jnp.sum(z2, axis=-1, keepdims=True)
    s_lo = jnp.sum(jnp.where(lo, z2, 0.0), axis=-1, keepdims=True)
    ms = jnp.where(lo, s_lo, s_all - s_lo) * (1.0 / HEAD_DIM)
    return (z * lax.rsqrt(ms + NORM_EPS)) * gain


def _pad_halves(z, lo):
    sw = pltpu.roll(z, HEAD_DIM, 1)
    zero = jnp.zeros_like(z)
    parts = [jnp.where(lo, z, zero), jnp.where(lo, zero, sw),
             jnp.where(lo, sw, zero), jnp.where(lo, zero, z)]
    return jnp.concatenate(parts, axis=1).astype(_BF16)


def _sigmoid(z):
    return 1.0 / (1.0 + jnp.exp(-z))


def _in_proj_kernel(xl_ref, xc_ref, mod_ref, g_ref, w_ref, rope_ref, qg_ref, kg_ref,
                    qa_ref, ka_ref, va_ref, qb_ref, kb_ref, vb_ref, ga_ref, gb_ref):
    x = _pick_stream(xl_ref, xc_ref)
    shift = mod_ref[:, 0:D_MODEL]
    scale = mod_ref[:, D_MODEL:2 * D_MODEL]
    h = _norm_modulate(x, g_ref[...], shift, scale).astype(_BF16)
    cos = rope_ref[:, 0:LANES]
    sin_up = rope_ref[:, LANES:2 * LANES]
    sin_dn = rope_ref[:, 2 * LANES:3 * LANES]
    lo = lax.broadcasted_iota(jnp.int32, (1, LANES), 1) < HEAD_DIM
    qscale = HEAD_DIM ** -0.5 * LOG2E

    def proj(c0, width):
        return jnp.dot(h, w_ref[:, c0:c0 + width], preferred_element_type=_F32)

    c = 0
    z = proj(c, Q_W)
    for j in range(Q_W // LANES):
        zj = z[:, j * LANES:(j + 1) * LANES]
        qa_ref[:, j * LANES:(j + 1) * LANES] = (_rope(zj, cos, sin_up, sin_dn) * qscale).astype(_BF16)
    c += Q_W
    ka_ref[...] = _pad_halves(_rope(proj(c, KV_W), cos, sin_up, sin_dn), lo)
    c += KV_W
    va_ref[...] = _pad_halves(proj(c, KV_W), lo)
    c += KV_W
    z = proj(c, Q_W)
    qg = qg_ref[...]
    for j in range(Q_W // LANES):
        zj = _head_rms(z[:, j * LANES:(j + 1) * LANES], qg, lo)
        qb_ref[:, j * LANES:(j + 1) * LANES] = (_rope(zj, cos, sin_up, sin_dn) * qscale).astype(_BF16)
    c += Q_W
    zk = _head_rms(proj(c, KV_W), kg_ref[...], lo)
    kb_ref[...] = _pad_halves(_rope(zk, cos, sin_up, sin_dn), lo)
    c += KV_W
    vb_ref[...] = _pad_halves(proj(c, KV_W), lo)
    c += KV_W
    ga_ref[...] = _sigmoid(proj(c, D_MODEL)).astype(_BF16)
    c += D_MODEL
    gb_ref[...] = _sigmoid(proj(c, D_MODEL)).astype(_BF16)


def _in_proj(x_lat, x_ctx, ctx_tile0, mod_l, g1, w_in, rope_tab, qg, kg):
    tile = lambda w: pl.BlockSpec((TM, w), lambda i: (i, 0))
    rope_spec = pl.BlockSpec(
        (TM, 3 * LANES), lambda i: (jnp.where(i < LAT_TILES, i % TILES_PER_SEQ, TILES_PER_SEQ), 0))
    widths = [Q_W, PAD_W, PAD_W, Q_W, PAD_W, PAD_W, D_MODEL, D_MODEL]
    return pl.pallas_call(
        _in_proj_kernel,
        grid=(ALL_TILES,),
        in_specs=[*_split_specs(D_MODEL, ctx_tile0), _mod_spec(), _resident((1, D_MODEL)),
                  _resident((D_MODEL, IN_COLS)), rope_spec,
                  _resident((1, LANES)), _resident((1, LANES))],
        out_specs=[tile(w) for w in widths],
        out_shape=[jax.ShapeDtypeStruct((N_TOK, w), _BF16) for w in widths],
        compiler_params=_params(("arbitrary",)),
        name="in_proj",
    )(x_lat, x_ctx, mod_l, g1, w_in, rope_tab, qg, kg)


def _stack_pairs(q, g):
    return jnp.concatenate([q[:, (2 * g) * LANES:(2 * g + 1) * LANES],
                            q[:, (2 * g + 1) * LANES:(2 * g + 2) * LANES]], axis=0)


def _fold_lanes(x, op):
    out = x[:, 0:LANES]
    for c in range(1, x.shape[1] // LANES):
        out = op(out, x[:, c * LANES:(c + 1) * LANES])
    return out


def _softmax_pv(scores, values, sink_col):
    s = scores[0] if len(scores) == 1 else jnp.concatenate(scores, axis=1)
    m = jnp.max(_fold_lanes(s, jnp.maximum), axis=-1, keepdims=True)
    if sink_col is not None:
        m = jnp.maximum(m, sink_col)
    p = jnp.exp2(s - m)
    den = jnp.sum(_fold_lanes(p, jnp.add), axis=-1, keepdims=True)
    if sink_col is not None:
        den = den + jnp.exp2(sink_col - m)
    pb = p.astype(_BF16)
    out, c0 = None, 0
    for v in values:
        o = jnp.dot(pb[:, c0:c0 + v.shape[0]], v, preferred_element_type=_F32)
        out = o if out is None else out + o
        c0 += v.shape[0]
    return out * (1.0 / den)


def _window_kernel(sink_ref, q_ref, k_ref, v_ref, kc_ref, vc_ref, o_ref):
    n = pl.program_id(1)
    nb = SEQ // BLOCK
    start = pl.multiple_of(jnp.clip(n - 1, 0, nb - 3) * BLOCK, BLOCK)
    rows = 2 * BLOCK
    r = lax.broadcasted_iota(jnp.int32, (rows, 3 * BLOCK), 0)
    col = lax.broadcasted_iota(jnp.int32, (rows, 3 * BLOCK), 1)
    rel = (start + col) - (n * BLOCK + (r & (BLOCK - 1)))
    bias = jnp.where(jnp.abs(rel) <= WINDOW, 0.0, NEG_INF)
    top = lax.broadcasted_iota(jnp.int32, (rows, 1), 0) < BLOCK
    q = q_ref[...]
    for g in range(N_KV):
        qs = _stack_pairs(q, g)
        acc = None
        for e in range(2):
            cs = slice((2 * g + e) * LANES, (2 * g + e + 1) * LANES)
            kw = k_ref[pl.ds(start, 3 * BLOCK), cs]
            vw = v_ref[pl.ds(start, 3 * BLOCK), cs]
            s_w = lax.dot_general(qs, kw, _NT, preferred_element_type=_F32) + bias
            s_c = lax.dot_general(qs, kc_ref[:, cs], _NT, preferred_element_type=_F32)
            sink_col = jnp.where(top, sink_ref[4 * g + e], sink_ref[4 * g + 2 + e]) * LOG2E
            o = _softmax_pv([s_w, s_c], [vw, vc_ref[:, cs]], sink_col)
            acc = o if acc is None else acc + o
        o_ref[:, (2 * g) * LANES:(2 * g + 1) * LANES] = acc[:BLOCK].astype(_BF16)
        o_ref[:, (2 * g + 1) * LANES:(2 * g + 2) * LANES] = acc[BLOCK:].astype(_BF16)


def _window_attention(sink, qa, ka, va):
    nb = SEQ // BLOCK
    lat_kv = pl.BlockSpec((SEQ, PAD_W), lambda b, n: (b, 0))
    ctx_kv = pl.BlockSpec((CTX_LEN, PAD_W), lambda b, n: (N_LAT // CTX_LEN + b, 0))
    q_spec = pl.BlockSpec((BLOCK, Q_W), lambda b, n: (b * nb + n, 0))
    return pl.pallas_call(
        _window_kernel,
        grid=(BATCH, nb),
        in_specs=[pl.BlockSpec(memory_space=pltpu.SMEM), q_spec, lat_kv, lat_kv, ctx_kv, ctx_kv],
        out_specs=q_spec,
        out_shape=jax.ShapeDtypeStruct((N_LAT, Q_W), _BF16),
        compiler_params=_params(("arbitrary", "arbitrary")),
        name="window_attn",
    )(sink, qa, ka, va, ka, va)


def _dense_kernel(q_ref, k_ref, v_ref, kc_ref, vc_ref, o_ref, m_ref, l_ref, acc_ref):
    tq = q_ref.shape[0]
    q = q_ref[...]
    qs = jnp.concatenate([q[:, 0:LANES], q[:, LANES:2 * LANES]], axis=0)
    m_ref[...] = jnp.full(m_ref.shape, -jnp.inf, _F32)
    l_ref[...] = jnp.zeros(l_ref.shape, _F32)
    acc_ref[...] = jnp.zeros(acc_ref.shape, _F32)

    def update(e, k, v):
        s = lax.dot_general(qs, k, _NT, preferred_element_type=_F32)
        m_prev = m_ref[e]
        m_new = jnp.maximum(m_prev, jnp.max(s, axis=-1, keepdims=True))
        alpha = jnp.exp2(m_prev - m_new)
        p = jnp.exp2(s - m_new[:, 0:1])
        l_ref[e] = alpha * l_ref[e] + jnp.sum(p, axis=-1, keepdims=True)
        acc_ref[e] = alpha * acc_ref[e] + jnp.dot(p.astype(_BF16), v, preferred_element_type=_F32)
        m_ref[e] = m_new

    def chunk(j, carry):
        start = pl.multiple_of(j * TK_DENSE, TK_DENSE)
        for e in range(2):
            cs = slice(e * LANES, (e + 1) * LANES)
            update(e, k_ref[pl.ds(start, TK_DENSE), cs], v_ref[pl.ds(start, TK_DENSE), cs])
        return carry

    lax.fori_loop(0, SEQ // TK_DENSE, chunk, 0)
    for e in range(2):
        cs = slice(e * LANES, (e + 1) * LANES)
        update(e, kc_ref[:, cs], vc_ref[:, cs])
    out = acc_ref[0] / l_ref[0] + acc_ref[1] / l_ref[1]
    o_ref[:, 0:LANES] = out[:tq].astype(_BF16)
    o_ref[:, LANES:2 * LANES] = out[tq:].astype(_BF16)


def _dense_bounded_kernel(q_ref, k_ref, v_ref, kc_ref, vc_ref, o_ref, l_ref, acc_ref):
    tq = q_ref.shape[0]
    q = q_ref[...]
    qs = jnp.concatenate([q[:, 0:LANES], q[:, LANES:2 * LANES]], axis=0)
    l_ref[...] = jnp.zeros(l_ref.shape, _F32)
    acc_ref[...] = jnp.zeros(acc_ref.shape, _F32)

    def update(e, k, v):
        p = jnp.exp2(lax.dot_general(qs, k, _NT, preferred_element_type=_F32))
        l_ref[e] += _fold_lanes(p, jnp.add)
        acc_ref[e] += jnp.dot(p.astype(_BF16), v, preferred_element_type=_F32)

    def chunk(j, carry):
        for u in range(TK_DENSE // TK_SUB):
            start = pl.multiple_of(j * TK_DENSE + u * TK_SUB, TK_SUB)
            for e in range(2):
                cs = slice(e * LANES, (e + 1) * LANES)
                update(e, k_ref[pl.ds(start, TK_SUB), cs], v_ref[pl.ds(start, TK_SUB), cs])
        return carry

    lax.fori_loop(0, SEQ // TK_DENSE, chunk, 0)
    for e in range(2):
        cs = slice(e * LANES, (e + 1) * LANES)
        update(e, kc_ref[:, cs], vc_ref[:, cs])
    out = (acc_ref[0] / jnp.sum(l_ref[0], axis=-1, keepdims=True)
           + acc_ref[1] / jnp.sum(l_ref[1], axis=-1, keepdims=True))
    o_ref[:, 0:LANES] = out[:tq].astype(_BF16)
    o_ref[:, LANES:2 * LANES] = out[tq:].astype(_BF16)


def _dense_attention(qb, kb, vb, bounded):
    tq = TQ_DENSE if bounded else TQ_DENSE_UNBOUNDED
    nq = SEQ // tq
    gw = 2 * LANES
    lat_kv = pl.BlockSpec((SEQ, gw), lambda b, g, i: (b, g))
    ctx_kv = pl.BlockSpec((CTX_LEN, gw), lambda b, g, i: (N_LAT // CTX_LEN + b, g))
    q_spec = pl.BlockSpec((tq, gw), lambda b, g, i: (b * nq + i, g))
    stat = pltpu.VMEM((2, 2 * tq, LANES), _F32)
    return pl.pallas_call(
        _dense_bounded_kernel if bounded else _dense_kernel,
        grid=(BATCH, N_KV, nq),
        in_specs=[q_spec, lat_kv, lat_kv, ctx_kv, ctx_kv],
        out_specs=q_spec,
        out_shape=jax.ShapeDtypeStruct((N_LAT, Q_W), _BF16),
        scratch_shapes=[stat, stat] if bounded else [stat, stat, stat],
        compiler_params=_params(("arbitrary", "arbitrary", "arbitrary")),
        name="dense_attn_bounded" if bounded else "dense_attn",
    )(qb, kb, vb, kb, vb)


def _ctx_kernel(sink_ref, qa_ref, ka_ref, va_ref, qb_ref, kb_ref, vb_ref, oa_ref, ob_ref):
    top = lax.broadcasted_iota(jnp.int32, (2 * CTX_LEN, 1), 0) < CTX_LEN
    for q_ref, k_ref, v_ref, o_ref, use_sink in ((qa_ref, ka_ref, va_ref, oa_ref, True),
                                                 (qb_ref, kb_ref, vb_ref, ob_ref, False)):
        q = q_ref[...]
        for g in range(N_KV):
            qs = _stack_pairs(q, g)
            acc = None
            for e in range(2):
                cs = slice((2 * g + e) * LANES, (2 * g + e + 1) * LANES)
                s = lax.dot_general(qs, k_ref[:, cs], _NT, preferred_element_type=_F32)
                sink_col = None
                if use_sink:
                    sink_col = jnp.where(top, sink_ref[4 * g + e], sink_ref[4 * g + 2 + e]) * LOG2E
                o = _softmax_pv([s], [v_ref[:, cs]], sink_col)
                acc = o if acc is None else acc + o
            o_ref[:, (2 * g) * LANES:(2 * g + 1) * LANES] = acc[:CTX_LEN].astype(_BF16)
            o_ref[:, (2 * g + 1) * LANES:(2 * g + 2) * LANES] = acc[CTX_LEN:].astype(_BF16)


def _ctx_attention(sink, qa, ka, va, qb, kb, vb):
    row = lambda w: pl.BlockSpec((CTX_LEN, w), lambda b: (N_LAT // CTX_LEN + b, 0))
    out = pl.BlockSpec((CTX_LEN, Q_W), lambda b: (b, 0))
    return pl.pallas_call(
        _ctx_kernel,
        grid=(BATCH,),
        in_specs=[pl.BlockSpec(memory_space=pltpu.SMEM),
                  row(Q_W), row(PAD_W), row(PAD_W), row(Q_W), row(PAD_W), row(PAD_W)],
        out_specs=[out, out],
        out_shape=[jax.ShapeDtypeStruct((N_CTX, Q_W), _BF16)] * 2,
        compiler_params=_params(("arbitrary",)),
        name="ctx_attn",
    )(sink, qa, ka, va, qb, kb, vb)


def _merge_kernel(xl_ref, xc_ref, mod_ref, yal_ref, yac_ref, ybl_ref, ybc_ref, ga_ref, gb_ref,
                  wa_ref, wb_ref, wo_ref, o_ref):
    ya = _pick_stream(yal_ref, yac_ref)
    yb = _pick_stream(ybl_ref, ybc_ref)
    pa = jnp.dot(ya, wa_ref[...], preferred_element_type=_F32)
    pb = jnp.dot(yb, wb_ref[...], preferred_element_type=_F32)
    m = ga_ref[...].astype(_F32) * pa + gb_ref[...].astype(_F32) * pb
    y = jnp.dot(m.astype(_BF16), wo_ref[...], preferred_element_type=_F32)
    gate = mod_ref[:, 2 * D_MODEL:3 * D_MODEL]
    o_ref[...] = _pick_stream(xl_ref, xc_ref) + gate * y


def _merge(x_lat, x_ctx, ctx_tile0, mod_l, ya, yac, yb, ybc, ga, gb, w_pa, w_pb, w_o, n_tiles):
    tile = lambda w: pl.BlockSpec((TM, w), lambda i: (i, 0))
    return pl.pallas_call(
        _merge_kernel,
        grid=(n_tiles,),
        in_specs=[*_split_specs(D_MODEL, ctx_tile0), _mod_spec(),
                  *_split_specs(Q_W, 0), *_split_specs(Q_W, 0), tile(D_MODEL), tile(D_MODEL),
                  _resident((Q_W, D_MODEL)), _resident((Q_W, D_MODEL)),
                  _resident((D_MODEL, D_MODEL))],
        out_specs=tile(D_MODEL),
        out_shape=jax.ShapeDtypeStruct((n_tiles * TM, D_MODEL), _F32),
        compiler_params=_params(("arbitrary",)),
        name="merge",
    )(x_lat, x_ctx, mod_l, ya, yac, yb, ybc, ga, gb, w_pa, w_pb, w_o)


def _ffn_kernel(x_ref, mod_ref, g_ref, wg_ref, wu_ref, wd_ref, gf_ref, o_ref, *, final_norm):
    x = x_ref[...]
    shift = mod_ref[:, 3 * D_MODEL:4 * D_MODEL]
    scale = mod_ref[:, 4 * D_MODEL:5 * D_MODEL]
    gate = mod_ref[:, 5 * D_MODEL:6 * D_MODEL]
    h = _norm_modulate(x, g_ref[...], shift, scale).astype(_BF16)
    a = jnp.dot(h, wg_ref[...], preferred_element_type=_F32)
    u = jnp.dot(h, wu_ref[...], preferred_element_type=_F32)
    act = ((a / (1.0 + jnp.exp(-a))) * u).astype(_BF16)
    y = x + gate * jnp.dot(act, wd_ref[...], preferred_element_type=_F32)
    if final_norm:
        ms = jnp.mean(y * y, axis=-1, keepdims=True)
        y = (y * lax.rsqrt(ms + NORM_EPS)) * gf_ref[...]
    o_ref[...] = y


def _ffn(x_all, mod_l, g2, w_gate, w_up, w_down, gf, n_tiles, final_norm):
    tile = pl.BlockSpec((TM, D_MODEL), lambda i: (i, 0))
    return pl.pallas_call(
        functools.partial(_ffn_kernel, final_norm=final_norm),
        grid=(n_tiles,),
        in_specs=[tile, _mod_spec(), _resident((1, D_MODEL)),
                  _resident((D_MODEL, D_FF)), _resident((D_MODEL, D_FF)),
                  _resident((D_FF, D_MODEL)), _resident((1, D_MODEL))],
        out_specs=tile,
        out_shape=jax.ShapeDtypeStruct((n_tiles * TM, D_MODEL), _F32),
        compiler_params=_params(("arbitrary",)),
        name="ffn",
    )(x_all, mod_l, g2, w_gate, w_up, w_down, gf)


def _rope_table():
    rows = SEQ // GRID_W
    row = jnp.repeat(jnp.arange(rows, dtype=_F32), GRID_W)
    col = jnp.tile(jnp.arange(GRID_W, dtype=_F32), rows)
    axis_dims = HEAD_DIM // 2
    inv = ROPE_THETA ** (-jnp.arange(0, axis_dims, 2, dtype=_F32) / axis_dims)
    ang = jnp.stack([row[:, None] * inv, col[:, None] * inv], axis=1)
    cos, sin = jnp.cos(ang), jnp.sin(ang)
    lane = np.arange(LANES)
    axis = (lane % HEAD_DIM) // axis_dims
    upper = ((lane % axis_dims) // (axis_dims // 2)) == 1
    j = lane % (axis_dims // 2)
    cos_l = cos[:, axis, j]
    sin_l = sin[:, axis, j]
    sin_up = jnp.where(upper[None, :], sin_l, 0.0)
    sin_dn = jnp.where(upper[None, :], 0.0, -sin_l)
    lat = jnp.concatenate([cos_l, sin_up, sin_dn], axis=1)
    ident = jnp.concatenate([jnp.ones((TM, LANES), _F32), jnp.zeros((TM, 2 * LANES), _F32)], axis=1)
    return jnp.concatenate([lat, ident], axis=0)


def kernel(x, c, ctx, c_ctx, w_ada, b_ada, norm1_g, norm2_g, w_in, q_norm_g, k_norm_g, sink_a,
           w_proj_a, w_proj_b, w_out, w_ffn_gate, w_ffn_up, w_ffn_down, final_norm_g):
    assert x.shape == (BATCH, SEQ, D_MODEL) and ctx.shape == (BATCH, CTX_LEN, D_MODEL)
    x_lat, x_ctx, ctx_tile0 = x.reshape(N_LAT, D_MODEL), ctx.reshape(N_CTX, D_MODEL), 0
    cc = jnp.concatenate([c, c_ctx[None, :], jnp.zeros((MOD_ROWS - BATCH - 1, D_MODEL), _F32)], axis=0)
    mod = _modulation(cc, w_ada, b_ada).reshape(DEPTH, MOD_ROWS, 1, 6 * D_MODEL)
    rope_tab = _rope_table()
    gf = final_norm_g.reshape(1, D_MODEL)
    out = None
    for l in range(DEPTH):
        last = l == DEPTH - 1
        bf = lambda w: w[l].astype(_BF16)
        qa, ka, va, qb, kb, vb, ga, gb = _in_proj(
            x_lat, x_ctx, ctx_tile0, mod[l], norm1_g[l].reshape(1, D_MODEL), bf(w_in), rope_tab,
            jnp.tile(q_norm_g[l], 2).reshape(1, LANES), jnp.tile(k_norm_g[l], 2).reshape(1, LANES))
        ya = _window_attention(sink_a[l], qa, ka, va)
        score_bound = HEAD_DIM ** 0.5 * jnp.max(jnp.abs(q_norm_g[l])) * jnp.max(jnp.abs(k_norm_g[l]))
        yb = lax.cond(score_bound <= SCORE_BOUND,
                      functools.partial(_dense_attention, bounded=True),
                      functools.partial(_dense_attention, bounded=False), qb, kb, vb)
        n_tiles = LAT_TILES if last else ALL_TILES
        yac, ybc = (ya, yb) if last else _ctx_attention(sink_a[l], qa, ka, va, qb, kb, vb)
        x_mid = _merge(x_lat, x_ctx, ctx_tile0, mod[l], ya, yac, yb, ybc, ga, gb,
                       bf(w_proj_a), bf(w_proj_b), bf(w_out), n_tiles)
        x_new = _ffn(x_mid, mod[l], norm2_g[l].reshape(1, D_MODEL), bf(w_ffn_gate), bf(w_ffn_up),
                     bf(w_ffn_down), gf, n_tiles, last)
        if last:
            out = x_new
        else:
            x_lat, x_ctx, ctx_tile0 = x_new, x_new, LAT_TILES
    return out.reshape(BATCH, SEQ, D_MODEL)
```

```python
import functools

import jax
import jax.numpy as jnp
import numpy as np
from jax import lax
from jax.experimental import pallas as pl
from jax.experimental.pallas import tpu as pltpu

D_MODEL = 1024
BATCH = 4
SEQ = 4096
DEPTH = 4
CTX_LEN = 256
GRID_W = 64
HEAD_DIM = 64
N_HEADS = 8
N_KV = 2
WINDOW = 128
BLOCK = 128
D_FF = 2816
ROPE_THETA = 10000.0
NORM_EPS = 1e-6
NEG_INF = -1e30
Q_W = N_HEADS * HEAD_DIM
KV_W = N_KV * HEAD_DIM
IN_COLS = 2 * (Q_W + 2 * KV_W) + 2 * D_MODEL
LANES = 128
PAD_W = N_KV * LANES
QS_W = 2 * Q_W
HEADS_PER_KV = N_HEADS // N_KV
HEAD_ORDER = (0, 2, 1, 3)

N_LAT = BATCH * SEQ
N_CTX = BATCH * CTX_LEN
N_TOK = N_LAT + N_CTX
MOD_ROWS = 8
CTX_MOD_ROW = BATCH

TM = 512
LAT_TILES = N_LAT // TM
ALL_TILES = N_TOK // TM
TILES_PER_SEQ = SEQ // TM
TQ_DENSE = 512
TQ_DENSE_UNBOUNDED = 128
TQ_WINDOW = 512
WIN_KEYS = BLOCK + 2 * WINDOW
TK_DENSE = 2048
TK_UNBOUNDED = 1024
TK_SUB = 256
LOG2E = 1.4426950408889634
SCORE_BOUND = 60.0
VMEM_LIMIT = 56 * 1024 * 1024

_F32 = jnp.float32
_BF16 = jnp.bfloat16
_NT = (((1,), (1,)), ((), ()))


def _params(sem):
    return pltpu.CompilerParams(dimension_semantics=sem, vmem_limit_bytes=VMEM_LIMIT)


def _resident(shape, layer=None):
    nd = len(shape)
    if layer is None:
        return pl.BlockSpec(shape, lambda *_: (0,) * nd, pipeline_mode=pl.Buffered(1))
    return pl.BlockSpec((None,) + tuple(shape), lambda *_: (layer,) + (0,) * nd,
                        pipeline_mode=pl.Buffered(1))


def _mod_row_of_tile(i):
    return jnp.minimum(i // TILES_PER_SEQ, CTX_MOD_ROW)


def _mod_spec():
    return pl.BlockSpec((None, 1, 6 * D_MODEL), lambda i: (_mod_row_of_tile(i), 0, 0))


def _split_specs(width, ctx_tile0):
    lat = pl.BlockSpec((TM, width), lambda i: (jnp.minimum(i, LAT_TILES - 1), 0))
    ctx = pl.BlockSpec((TM, width), lambda i: (jnp.maximum(i - LAT_TILES, 0) + ctx_tile0, 0))
    return lat, ctx


def _pick_stream(lat_ref, ctx_ref):
    return jnp.where(pl.program_id(0) < LAT_TILES, lat_ref[...], ctx_ref[...])


def _mod_kernel(c_ref, w_ref, b_ref, o_ref):
    c = c_ref[...]
    a = c / (1.0 + jnp.exp(-c))
    o_ref[...] = jnp.dot(a, w_ref[...], preferred_element_type=_F32) + b_ref[...]


def _modulation(cc, w_ada, b_ada):
    tn = 1536
    return pl.pallas_call(
        _mod_kernel,
        grid=(DEPTH, 6 * D_MODEL // tn),
        in_specs=[
            pl.BlockSpec((MOD_ROWS, D_MODEL), lambda l, j: (0, 0)),
            pl.BlockSpec((None, D_MODEL, tn), lambda l, j: (l, 0, j)),
            pl.BlockSpec((None, 1, tn), lambda l, j: (l, 0, j)),
        ],
        out_specs=pl.BlockSpec((None, MOD_ROWS, tn), lambda l, j: (l, 0, j)),
        out_shape=jax.ShapeDtypeStruct((DEPTH, MOD_ROWS, 6 * D_MODEL), _F32),
        compiler_params=_params(("arbitrary", "arbitrary")),
        name="adaln_mod",
    )(cc, w_ada, b_ada.reshape(DEPTH, 1, 6 * D_MODEL))


def _norm_modulate(x, g, shift, scale):
    ms = jnp.mean(x * x, axis=-1, keepdims=True)
    return (x * lax.rsqrt(ms + NORM_EPS)) * (g * (1.0 + scale)) + shift


def _rope(z, cos, sin_up, sin_dn):
    return z * cos + pltpu.roll(z, 16, 1) * sin_up + pltpu.roll(z, LANES - 16, 1) * sin_dn


def _head_rms(z, gain, lo):
    z2 = z * z
    s_all = jnp.sum(z2, axis=-1, keepdims=True)
    s_lo = jnp.sum(jnp.where(lo, z2, 0.0), axis=-1, keepdims=True)
    ms = jnp.where(lo, s_lo, s_all - s_lo) * (1.0 / HEAD_DIM)
    return (z * lax.rsqrt(ms + NORM_EPS)) * gain


def _pad_heads(z, lo, fill):
    sw = pltpu.roll(z, HEAD_DIM, 1)
    return jnp.concatenate([jnp.where(lo, z, fill), jnp.where(lo, sw, fill)], axis=1).astype(_BF16)


def _store_q(o_ref, j, e):
    g, t = divmod(j, 2)
    o_ref[:, (4 * g + t) * LANES:(4 * g + t + 1) * LANES] = e.astype(_BF16)
    o_ref[:, (4 * g + 2 + t) * LANES:(4 * g + 3 + t) * LANES] = pltpu.roll(e, HEAD_DIM, 1).astype(_BF16)


def _sigmoid(z):
    return 1.0 / (1.0 + jnp.exp(-z))


def _in_proj_kernel(xl_ref, xc_ref, mod_ref, g_ref, w_ref, rope_ref, qg_ref, kg_ref,
                    qa_ref, ka_ref, va_ref, qb_ref, kb_ref, vb_ref, ga_ref, gb_ref):
    x = _pick_stream(xl_ref, xc_ref)
    shift = mod_ref[:, 0:D_MODEL]
    scale = mod_ref[:, D_MODEL:2 * D_MODEL]
    h = _norm_modulate(x, g_ref[...], shift, scale).astype(_BF16)
    cos = rope_ref[:, 0:LANES]
    sin_up = rope_ref[:, LANES:2 * LANES]
    sin_dn = rope_ref[:, 2 * LANES:3 * LANES]
    lo = lax.broadcasted_iota(jnp.int32, (1, LANES), 1) < HEAD_DIM
    qscale = HEAD_DIM ** -0.5 * LOG2E

    def proj(c0, width):
        return jnp.dot(h, w_ref[:, c0:c0 + width], preferred_element_type=_F32)

    c = 0
    z = proj(c, Q_W)
    for j in range(Q_W // LANES):
        zj = z[:, j * LANES:(j + 1) * LANES]
        _store_q(qa_ref, j, _rope(zj, cos, sin_up, sin_dn) * qscale)
    c += Q_W
    z = proj(c, 2 * KV_W)
    ka_ref[...] = _pad_heads(_rope(z[:, 0:KV_W], cos, sin_up, sin_dn), lo, 0.0)
    va_ref[...] = _pad_heads(z[:, KV_W:2 * KV_W], lo, 1.0)
    c += 2 * KV_W
    z = proj(c, Q_W)
    qg = qg_ref[...]
    for j in range(Q_W // LANES):
        zj = _head_rms(z[:, j * LANES:(j + 1) * LANES], qg, lo)
        _store_q(qb_ref, j, _rope(zj, cos, sin_up, sin_dn) * qscale)
    c += Q_W
    z = proj(c, 2 * KV_W)
    zk = _head_rms(z[:, 0:KV_W], kg_ref[...], lo)
    kb_ref[...] = _pad_heads(_rope(zk, cos, sin_up, sin_dn), lo, 0.0)
    vb_ref[...] = _pad_heads(z[:, KV_W:2 * KV_W], lo, 1.0)
    c += 2 * KV_W
    ga_ref[...] = _sigmoid(proj(c, D_MODEL)).astype(_BF16)
    c += D_MODEL
    gb_ref[...] = _sigmoid(proj(c, D_MODEL)).astype(_BF16)


def _in_proj(layer, x_lat, x_ctx, ctx_tile0, mod_l, g1, w_in, rope_tab, qg, kg):
    tile = lambda w: pl.BlockSpec((TM, w), lambda i: (i, 0))
    rope_spec = pl.BlockSpec(
        (TM, 3 * LANES), lambda i: (jnp.where(i < LAT_TILES, i % TILES_PER_SEQ, TILES_PER_SEQ), 0))
    widths = [QS_W, PAD_W, PAD_W, QS_W, PAD_W, PAD_W, D_MODEL, D_MODEL]
    return pl.pallas_call(
        _in_proj_kernel,
        grid=(ALL_TILES,),
        in_specs=[*_split_specs(D_MODEL, ctx_tile0), _mod_spec(), _resident((1, D_MODEL), layer),
                  _resident((D_MODEL, IN_COLS), layer), rope_spec,
                  _resident((1, LANES), layer), _resident((1, LANES), layer)],
        out_specs=[tile(w) for w in widths],
        out_shape=[jax.ShapeDtypeStruct((N_TOK, w), _BF16) for w in widths],
        compiler_params=_params(("arbitrary",)),
        name="in_proj",
    )(x_lat, x_ctx, mod_l, g1, w_in, rope_tab, qg, kg)


def _stack_heads(q, g):
    return jnp.concatenate([q[:, (4 * g + i) * LANES:(4 * g + i + 1) * LANES]
                            for i in range(HEADS_PER_KV)], axis=0)


def _store_heads(o_ref, g, o, extra_den=None):
    tq = o.shape[0] // HEADS_PER_KV
    lo = lax.broadcasted_iota(jnp.int32, (1, LANES), 1) < HEAD_DIM
    sw = pltpu.roll(o, HEAD_DIM, 1)
    den_lo, den_hi = (sw, o) if extra_den is None else (sw + extra_den, o + extra_den)
    for t in range(2):
        a = slice(t * tq, (t + 1) * tq)
        b = slice((2 + t) * tq, (3 + t) * tq)
        pair = jnp.where(lo, o[a] / den_lo[a], sw[b] / den_hi[b])
        o_ref[:, (2 * g + t) * LANES:(2 * g + t + 1) * LANES] = pair.astype(_BF16)


def _sink_column(sink_ref, g, tq):
    blk = lax.broadcasted_iota(jnp.int32, (HEADS_PER_KV * tq, 1), 0) // tq
    col = jnp.full((HEADS_PER_KV * tq, 1), sink_ref[HEADS_PER_KV * g + HEAD_ORDER[0]], _F32)
    for i in range(1, HEADS_PER_KV):
        col = jnp.where(blk == i, sink_ref[HEADS_PER_KV * g + HEAD_ORDER[i]], col)
    return col * LOG2E


def _fold_lanes(x, op):
    out = x[:, 0:LANES]
    for c in range(1, x.shape[1] // LANES):
        out = op(out, x[:, c * LANES:(c + 1) * LANES])
    return out


def _softmax_pv(scores, values, sink_col):
    s = scores[0] if len(scores) == 1 else jnp.concatenate(scores, axis=1)
    m = jnp.max(_fold_lanes(s, jnp.maximum), axis=-1, keepdims=True)
    if sink_col is not None:
        m = jnp.maximum(m, sink_col)
    pb = jnp.exp2(s - m).astype(_BF16)
    out, c0 = None, 0
    for v in values:
        o = jnp.dot(pb[:, c0:c0 + v.shape[0]], v, preferred_element_type=_F32)
        out = o if out is None else out + o
        c0 += v.shape[0]
    return out, (None if sink_col is None else jnp.exp2(sink_col - m))


def _window_kernel(sink_ref, q_ref, k_ref, v_ref, kc_ref, vc_ref, o_ref):
    j = pl.program_id(1)
    r = lax.broadcasted_iota(jnp.int32, (BLOCK, WIN_KEYS), 0)
    col = lax.broadcasted_iota(jnp.int32, (BLOCK, WIN_KEYS), 1)
    for blk in range(TQ_WINDOW // BLOCK):
        q0 = j * TQ_WINDOW + blk * BLOCK
        start = pl.multiple_of(jnp.clip(q0 - WINDOW, 0, SEQ - WIN_KEYS), BLOCK)
        bias = jnp.where(jnp.abs((start + col) - (q0 + r)) <= WINDOW, 0.0, NEG_INF)
        rows = slice(blk * BLOCK, (blk + 1) * BLOCK)
        q = q_ref[rows, :]
        for g in range(N_KV):
            qs = _stack_heads(q, g)
            cs = slice(g * LANES, (g + 1) * LANES)
            kw = k_ref[pl.ds(start, WIN_KEYS), cs]
            vw = v_ref[pl.ds(start, WIN_KEYS), cs]
            s_w = lax.dot_general(qs, kw, _NT, preferred_element_type=_F32)
            s_w = jnp.concatenate(
                [s_w[i * BLOCK:(i + 1) * BLOCK] + bias for i in range(HEADS_PER_KV)], axis=0)
            s_c = lax.dot_general(qs, kc_ref[:, cs], _NT, preferred_element_type=_F32)
            o, sink_term = _softmax_pv([s_w, s_c], [vw, vc_ref[:, cs]],
                                       _sink_column(sink_ref, g, BLOCK))
            _store_heads(o_ref.at[rows, :], g, o, sink_term)


def _window_attention(sink, qa, ka, va):
    nq = SEQ // TQ_WINDOW
    lat_kv = pl.BlockSpec((SEQ, PAD_W), lambda b, j: (b, 0))
    ctx_kv = pl.BlockSpec((CTX_LEN, PAD_W), lambda b, j: (N_LAT // CTX_LEN + b, 0))
    return pl.pallas_call(
        _window_kernel,
        grid=(BATCH, nq),
        in_specs=[pl.BlockSpec(memory_space=pltpu.SMEM),
                  pl.BlockSpec((TQ_WINDOW, QS_W), lambda b, j: (b * nq + j, 0)),
                  lat_kv, lat_kv, ctx_kv, ctx_kv],
        out_specs=pl.BlockSpec((TQ_WINDOW, Q_W), lambda b, j: (b * nq + j, 0)),
        out_shape=jax.ShapeDtypeStruct((N_LAT, Q_W), _BF16),
        compiler_params=_params(("arbitrary", "arbitrary")),
        name="window_attn",
    )(sink, qa, ka, va, ka, va)


def _dense_kernel(q_ref, k_ref, v_ref, kc_ref, vc_ref, o_ref, acc_ref, m_ref):
    qs = _stack_heads(q_ref[...], 0)
    m_ref[...] = jnp.full(m_ref.shape, -jnp.inf, _F32)
    acc_ref[...] = jnp.zeros(acc_ref.shape, _F32)

    def update(k, v):
        s = lax.dot_general(qs, k, _NT, preferred_element_type=_F32)
        m_prev = m_ref[...]
        m_new = jnp.maximum(m_prev, jnp.max(s, axis=-1, keepdims=True))
        p = jnp.exp2(s - m_new[:, 0:1])
        acc_ref[...] = (jnp.exp2(m_prev - m_new) * acc_ref[...]
                        + jnp.dot(p.astype(_BF16), v, preferred_element_type=_F32))
        m_ref[...] = m_new

    def chunk(j, carry):
        start = pl.multiple_of(j * TK_UNBOUNDED, TK_UNBOUNDED)
        update(k_ref[pl.ds(start, TK_UNBOUNDED), :], v_ref[pl.ds(start, TK_UNBOUNDED), :])
        return carry

    lax.fori_loop(0, SEQ // TK_UNBOUNDED, chunk, 0)
    update(kc_ref[...], vc_ref[...])
    _store_heads(o_ref, 0, acc_ref[...])


def _dense_bounded_kernel(q_ref, k_ref, v_ref, kc_ref, vc_ref, o_ref, acc_ref):
    qs = _stack_heads(q_ref[...], 0)

    def term(k, v):
        p = jnp.exp2(lax.dot_general(qs, k, _NT, preferred_element_type=_F32))
        return jnp.dot(p.astype(_BF16), v, preferred_element_type=_F32)

    acc_ref[...] = term(kc_ref[...], vc_ref[...])

    def chunk(j, carry):
        for u in range(TK_DENSE // TK_SUB):
            start = pl.multiple_of(j * TK_DENSE + u * TK_SUB, TK_SUB)
            acc_ref[...] += term(k_ref[pl.ds(start, TK_SUB), :], v_ref[pl.ds(start, TK_SUB), :])
        return carry

    lax.fori_loop(0, SEQ // TK_DENSE, chunk, 0)
    _store_heads(o_ref, 0, acc_ref[...])


def _dense_attention(qb, kb, vb, bounded):
    tq = TQ_DENSE if bounded else TQ_DENSE_UNBOUNDED
    nq = SEQ // tq
    lat_kv = pl.BlockSpec((SEQ, LANES), lambda b, g, i: (b, g))
    ctx_kv = pl.BlockSpec((CTX_LEN, LANES), lambda b, g, i: (N_LAT // CTX_LEN + b, g))
    stat = pltpu.VMEM((HEADS_PER_KV * tq, LANES), _F32)
    return pl.pallas_call(
        _dense_bounded_kernel if bounded else _dense_kernel,
        grid=(BATCH, N_KV, nq),
        in_specs=[pl.BlockSpec((tq, HEADS_PER_KV * LANES), lambda b, g, i: (b * nq + i, g)),
                  lat_kv, lat_kv, ctx_kv, ctx_kv],
        out_specs=pl.BlockSpec((tq, 2 * LANES), lambda b, g, i: (b * nq + i, g)),
        out_shape=jax.ShapeDtypeStruct((N_LAT, Q_W), _BF16),
        scratch_shapes=[stat] if bounded else [stat, stat],
        compiler_params=_params(("arbitrary", "arbitrary", "arbitrary")),
        name="dense_attn_bounded" if bounded else "dense_attn",
    )(qb, kb, vb, kb, vb)


def _ctx_kernel(sink_ref, qa_ref, ka_ref, va_ref, qb_ref, kb_ref, vb_ref, oa_ref, ob_ref):
    for q_ref, k_ref, v_ref, o_ref, use_sink in ((qa_ref, ka_ref, va_ref, oa_ref, True),
                                                 (qb_ref, kb_ref, vb_ref, ob_ref, False)):
        for blk in range(CTX_LEN // BLOCK):
            rows = slice(blk * BLOCK, (blk + 1) * BLOCK)
            q = q_ref[rows, :]
            for g in range(N_KV):
                cs = slice(g * LANES, (g + 1) * LANES)
                s = lax.dot_general(_stack_heads(q, g), k_ref[:, cs], _NT,
                                    preferred_element_type=_F32)
                sink_col = _sink_column(sink_ref, g, BLOCK) if use_sink else None
                _store_heads(o_ref.at[rows, :], g, *_softmax_pv([s], [v_ref[:, cs]], sink_col))


def _ctx_attention(sink, qa, ka, va, qb, kb, vb):
    row = lambda w: pl.BlockSpec((CTX_LEN, w), lambda b: (N_LAT // CTX_LEN + b, 0))
    out = pl.BlockSpec((CTX_LEN, Q_W), lambda b: (b, 0))
    return pl.pallas_call(
        _ctx_kernel,
        grid=(BATCH,),
        in_specs=[pl.BlockSpec(memory_space=pltpu.SMEM),
                  row(QS_W), row(PAD_W), row(PAD_W), row(QS_W), row(PAD_W), row(PAD_W)],
        out_specs=[out, out],
        out_shape=[jax.ShapeDtypeStruct((N_CTX, Q_W), _BF16)] * 2,
        compiler_params=_params(("arbitrary",)),
        name="ctx_attn",
    )(sink, qa, ka, va, qb, kb, vb)


def _merge_kernel(xl_ref, xc_ref, mod_ref, yal_ref, yac_ref, ybl_ref, ybc_ref, ga_ref, gb_ref,
                  wa_ref, wb_ref, wo_ref, o_ref):
    ya = _pick_stream(yal_ref, yac_ref)
    yb = _pick_stream(ybl_ref, ybc_ref)
    pa = jnp.dot(ya, wa_ref[...], preferred_element_type=_F32)
    pb = jnp.dot(yb, wb_ref[...], preferred_element_type=_F32)
    m = ga_ref[...].astype(_F32) * pa + gb_ref[...].astype(_F32) * pb
    y = jnp.dot(m.astype(_BF16), wo_ref[...], preferred_element_type=_F32)
    gate = mod_ref[:, 2 * D_MODEL:3 * D_MODEL]
    o_ref[...] = _pick_stream(xl_ref, xc_ref) + gate * y


def _merge(layer, x_lat, x_ctx, ctx_tile0, mod_l, ya, yac, yb, ybc, ga, gb, w_pa, w_pb, w_o, n_tiles):
    tile = lambda w: pl.BlockSpec((TM, w), lambda i: (i, 0))
    return pl.pallas_call(
        _merge_kernel,
        grid=(n_tiles,),
        in_specs=[*_split_specs(D_MODEL, ctx_tile0), _mod_spec(),
                  *_split_specs(Q_W, 0), *_split_specs(Q_W, 0), tile(D_MODEL), tile(D_MODEL),
                  _resident((Q_W, D_MODEL), layer), _resident((Q_W, D_MODEL), layer),
                  _resident((D_MODEL, D_MODEL), layer)],
        out_specs=tile(D_MODEL),
        out_shape=jax.ShapeDtypeStruct((n_tiles * TM, D_MODEL), _F32),
        compiler_params=_params(("arbitrary",)),
        name="merge",
    )(x_lat, x_ctx, mod_l, ya, yac, yb, ybc, ga, gb, w_pa, w_pb, w_o)


def _ffn_kernel(x_ref, mod_ref, g_ref, wg_ref, wu_ref, wd_ref, gf_ref, o_ref, *, final_norm):
    x = x_ref[...]
    shift = mod_ref[:, 3 * D_MODEL:4 * D_MODEL]
    scale = mod_ref[:, 4 * D_MODEL:5 * D_MODEL]
    gate = mod_ref[:, 5 * D_MODEL:6 * D_MODEL]
    h = _norm_modulate(x, g_ref[...], shift, scale).astype(_BF16)
    a = jnp.dot(h, wg_ref[...], preferred_element_type=_F32)
    u = jnp.dot(h, wu_ref[...], preferred_element_type=_F32)
    act = ((a / (1.0 + jnp.exp(-a))) * u).astype(_BF16)
    y = x + gate * jnp.dot(act, wd_ref[...], preferred_element_type=_F32)
    if final_norm:
        ms = jnp.mean(y * y, axis=-1, keepdims=True)
        y = (y * lax.rsqrt(ms + NORM_EPS)) * gf_ref[...]
    o_ref[...] = y


def _ffn(layer, x_all, mod_l, g2, w_gate, w_up, w_down, gf, n_tiles, final_norm):
    tile = pl.BlockSpec((TM, D_MODEL), lambda i: (i, 0))
    return pl.pallas_call(
        functools.partial(_ffn_kernel, final_norm=final_norm),
        grid=(n_tiles,),
        in_specs=[tile, _mod_spec(), _resident((1, D_MODEL), layer),
                  _resident((D_MODEL, D_FF), layer), _resident((D_MODEL, D_FF), layer),
                  _resident((D_FF, D_MODEL), layer), _resident((1, D_MODEL))],
        out_specs=tile,
        out_shape=jax.ShapeDtypeStruct((n_tiles * TM, D_MODEL), _F32),
        compiler_params=_params(("arbitrary",)),
        name="ffn",
    )(x_all, mod_l, g2, w_gate, w_up, w_down, gf)


def _rope_table():
    rows = SEQ // GRID_W
    row = jnp.repeat(jnp.arange(rows, dtype=_F32), GRID_W)
    col = jnp.tile(jnp.arange(GRID_W, dtype=_F32), rows)
    axis_dims = HEAD_DIM // 2
    inv = ROPE_THETA ** (-jnp.arange(0, axis_dims, 2, dtype=_F32) / axis_dims)
    ang = jnp.stack([row[:, None] * inv, col[:, None] * inv], axis=1)
    cos, sin = jnp.cos(ang), jnp.sin(ang)
    lane = np.arange(LANES)
    axis = (lane % HEAD_DIM) // axis_dims
    upper = ((lane % axis_dims) // (axis_dims // 2)) == 1
    j = lane % (axis_dims // 2)
    cos_l = cos[:, axis, j]
    sin_l = sin[:, axis, j]
    sin_up = jnp.where(upper[None, :], sin_l, 0.0)
    sin_dn = jnp.where(upper[None, :], 0.0, -sin_l)
    lat = jnp.concatenate([cos_l, sin_up, sin_dn], axis=1)
    ident = jnp.concatenate([jnp.ones((TM, LANES), _F32), jnp.zeros((TM, 2 * LANES), _F32)], axis=1)
    return jnp.concatenate([lat, ident], axis=0)


def kernel(x, c, ctx, c_ctx, w_ada, b_ada, norm1_g, norm2_g, w_in, q_norm_g, k_norm_g, sink_a,
           w_proj_a, w_proj_b, w_out, w_ffn_gate, w_ffn_up, w_ffn_down, final_norm_g):
    assert x.shape == (BATCH, SEQ, D_MODEL) and ctx.shape == (BATCH, CTX_LEN, D_MODEL)
    x_lat, x_ctx, ctx_tile0 = x.reshape(N_LAT, D_MODEL), ctx.reshape(N_CTX, D_MODEL), 0
    cc = jnp.concatenate([c, c_ctx[None, :], jnp.zeros((MOD_ROWS - BATCH - 1, D_MODEL), _F32)], axis=0)
    mod = _modulation(cc, w_ada, b_ada).reshape(DEPTH, MOD_ROWS, 1, 6 * D_MODEL)
    rope_tab = _rope_table()
    gf = final_norm_g.reshape(1, D_MODEL)
    w_in, w_proj_a, w_proj_b, w_out, w_ffn_gate, w_ffn_up, w_ffn_down = (
        w.astype(_BF16) for w in (w_in, w_proj_a, w_proj_b, w_out, w_ffn_gate, w_ffn_up, w_ffn_down))
    norm1_g = norm1_g.reshape(DEPTH, 1, D_MODEL)
    norm2_g = norm2_g.reshape(DEPTH, 1, D_MODEL)
    qg = jnp.tile(q_norm_g, (1, 2)).reshape(DEPTH, 1, LANES)
    kg = jnp.tile(k_norm_g, (1, 2)).reshape(DEPTH, 1, LANES)
    out = None
    for l in range(DEPTH):
        last = l == DEPTH - 1
        qa, ka, va, qb, kb, vb, ga, gb = _in_proj(
            l, x_lat, x_ctx, ctx_tile0, mod[l], norm1_g, w_in, rope_tab, qg, kg)
        ya = _window_attention(sink_a[l], qa, ka, va)
        score_bound = HEAD_DIM ** 0.5 * jnp.max(jnp.abs(q_norm_g[l])) * jnp.max(jnp.abs(k_norm_g[l]))
        yb = lax.cond(score_bound <= SCORE_BOUND,
                      functools.partial(_dense_attention, bounded=True),
                      functools.partial(_dense_attention, bounded=False), qb, kb, vb)
        n_tiles = LAT_TILES if last else ALL_TILES
        yac, ybc = (ya, yb) if last else _ctx_attention(sink_a[l], qa, ka, va, qb, kb, vb)
        x_mid = _merge(l, x_lat, x_ctx, ctx_tile0, mod[l], ya, yac, yb, ybc, ga, gb,
                       w_proj_a, w_proj_b, w_out, n_tiles)
        x_new = _ffn(l, x_mid, mod[l], norm2_g, w_ffn_gate, w_ffn_up, w_ffn_down, gf, n_tiles, last)
        if last:
            out = x_new
        else:
            x_lat, x_ctx, ctx_tile0 = x_new, x_new, LAT_TILES
    return out.reshape(BATCH, SEQ, D_MODEL)
```

```python
import functools

import jax
import jax.numpy as jnp
import numpy as np
from jax import lax
from jax.experimental import pallas as pl
from jax.experimental.pallas import tpu as pltpu

D_MODEL = 1024
BATCH = 4
SEQ = 4096
DEPTH = 4
CTX_LEN = 256
GRID_W = 64
HEAD_DIM = 64
N_HEADS = 8
N_KV = 2
WINDOW = 128
BLOCK = 128
D_FF = 2816
ROPE_THETA = 10000.0
NORM_EPS = 1e-6
NEG_INF = -1e30
Q_W = N_HEADS * HEAD_DIM
KV_W = N_KV * HEAD_DIM
IN_COLS = 2 * (Q_W + 2 * KV_W) + 2 * D_MODEL
LANES = 128
PAD_W = N_KV * LANES
HEADS_PER_KV = N_HEADS // N_KV
HEAD_ORDER = (0, 2, 1, 3)

N_LAT = BATCH * SEQ
N_CTX = BATCH * CTX_LEN
N_TOK = N_LAT + N_CTX
MOD_ROWS = 8
CTX_MOD_ROW = BATCH

TM = 1024
SUB_IN_PROJ, SUB_MERGE, SUB_FFN = 256, 512, 256
LAT_TILES = N_LAT // TM
ALL_TILES = N_TOK // TM
TILES_PER_SEQ = SEQ // TM
TQ_DENSE = 1024
TQ_DENSE_UNBOUNDED = 128
TQ_WINDOW = 512
WIN_KEYS = BLOCK + 2 * WINDOW
TK_DENSE = 2048
TK_UNBOUNDED = 1024
TK_SUB = 256
LOG2E = 1.4426950408889634
SCORE_BOUND = 60.0
VMEM_LIMIT = 56 * 1024 * 1024

_F32 = jnp.float32
_BF16 = jnp.bfloat16
_NT = (((1,), (1,)), ((), ()))


def _params(sem):
    return pltpu.CompilerParams(dimension_semantics=sem, vmem_limit_bytes=VMEM_LIMIT)


def _resident(shape, layer=None):
    nd = len(shape)
    if layer is None:
        return pl.BlockSpec(shape, lambda *_: (0,) * nd, pipeline_mode=pl.Buffered(1))
    return pl.BlockSpec((None,) + tuple(shape), lambda *_: (layer,) + (0,) * nd,
                        pipeline_mode=pl.Buffered(1))


def _mod_row_of_tile(i):
    return jnp.minimum(i // TILES_PER_SEQ, CTX_MOD_ROW)


def _mod_spec():
    return pl.BlockSpec((None, 1, 6 * D_MODEL), lambda i: (_mod_row_of_tile(i), 0, 0))


def _split_specs(width, ctx_tile0):
    lat = pl.BlockSpec((TM, width), lambda i: (jnp.minimum(i, LAT_TILES - 1), 0))
    ctx = pl.BlockSpec((TM, width), lambda i: (jnp.maximum(i - LAT_TILES, 0) + ctx_tile0, 0))
    return lat, ctx


def _pick_stream(lat_ref, ctx_ref, rows):
    return jnp.where(pl.program_id(0) < LAT_TILES, lat_ref[rows, :], ctx_ref[rows, :])


def _sub_tiles(sub_m):
    return [slice(s * sub_m, (s + 1) * sub_m) for s in range(TM // sub_m)]


def _mod_kernel(c_ref, w_ref, b_ref, o_ref):
    c = c_ref[...]
    a = c / (1.0 + jnp.exp(-c))
    o_ref[...] = jnp.dot(a, w_ref[...], preferred_element_type=_F32) + b_ref[...]


def _modulation(cc, w_ada, b_ada):
    tn = 1536
    return pl.pallas_call(
        _mod_kernel,
        grid=(DEPTH, 6 * D_MODEL // tn),
        in_specs=[
            pl.BlockSpec((MOD_ROWS, D_MODEL), lambda l, j: (0, 0)),
            pl.BlockSpec((None, D_MODEL, tn), lambda l, j: (l, 0, j)),
            pl.BlockSpec((None, 1, tn), lambda l, j: (l, 0, j)),
        ],
        out_specs=pl.BlockSpec((None, MOD_ROWS, tn), lambda l, j: (l, 0, j)),
        out_shape=jax.ShapeDtypeStruct((DEPTH, MOD_ROWS, 6 * D_MODEL), _F32),
        compiler_params=_params(("arbitrary", "arbitrary")),
        name="adaln_mod",
    )(cc, w_ada, b_ada.reshape(DEPTH, 1, 6 * D_MODEL))


def _norm_modulate(x, g, shift, scale):
    ms = jnp.mean(x * x, axis=-1, keepdims=True)
    return (x * lax.rsqrt(ms + NORM_EPS)) * (g * (1.0 + scale)) + shift


def _rope(z, cos, sin_up, sin_dn):
    return z * cos + pltpu.roll(z, 16, 1) * sin_up + pltpu.roll(z, LANES - 16, 1) * sin_dn


def _head_rms(z, gain, lo):
    z2 = z * z
    s_all = jnp.sum(z2, axis=-1, keepdims=True)
    s_lo = jnp.sum(jnp.where(lo, z2, 0.0), axis=-1, keepdims=True)
    ms = jnp.where(lo, s_lo, s_all - s_lo) * (1.0 / HEAD_DIM)
    return (z * lax.rsqrt(ms + NORM_EPS)) * gain


def _pad_heads(z, lo, fill):
    sw = pltpu.roll(z, HEAD_DIM, 1)
    return jnp.concatenate([jnp.where(lo, z, fill), jnp.where(lo, sw, fill)], axis=1).astype(_BF16)


def _sigmoid(z):
    return 1.0 / (1.0 + jnp.exp(-z))


def _in_proj_kernel(xl_ref, xc_ref, mod_ref, g_ref, w_ref, rope_ref, qg_ref, kg_ref,
                    qa_ref, ka_ref, va_ref, qb_ref, kb_ref, vb_ref, ga_ref, gb_ref):
    shift = mod_ref[:, 0:D_MODEL]
    scale = mod_ref[:, D_MODEL:2 * D_MODEL]
    lo = lax.broadcasted_iota(jnp.int32, (1, LANES), 1) < HEAD_DIM
    qscale = HEAD_DIM ** -0.5 * LOG2E
    qg = qg_ref[...]
    for rows in _sub_tiles(SUB_IN_PROJ):
        x = _pick_stream(xl_ref, xc_ref, rows)
        h = _norm_modulate(x, g_ref[...], shift, scale).astype(_BF16)
        cos = rope_ref[rows, 0:LANES]
        sin_up = rope_ref[rows, LANES:2 * LANES]
        sin_dn = rope_ref[rows, 2 * LANES:3 * LANES]

        def proj(c0, width):
            return jnp.dot(h, w_ref[:, c0:c0 + width], preferred_element_type=_F32)

        c = 0
        z = proj(c, Q_W)
        for j in range(Q_W // LANES):
            zj = _rope(z[:, j * LANES:(j + 1) * LANES], cos, sin_up, sin_dn)
            qa_ref[rows, j * LANES:(j + 1) * LANES] = (zj * qscale).astype(_BF16)
        c += Q_W
        z = proj(c, 2 * KV_W)
        ka_ref[rows, :] = _pad_heads(_rope(z[:, 0:KV_W], cos, sin_up, sin_dn), lo, 0.0)
        va_ref[rows, :] = _pad_heads(z[:, KV_W:2 * KV_W], lo, 1.0)
        c += 2 * KV_W
        z = proj(c, Q_W)
        for j in range(Q_W // LANES):
            zj = _rope(_head_rms(z[:, j * LANES:(j + 1) * LANES], qg, lo), cos, sin_up, sin_dn)
            qb_ref[rows, j * LANES:(j + 1) * LANES] = (zj * qscale).astype(_BF16)
        c += Q_W
        z = proj(c, 2 * KV_W)
        zk = _head_rms(z[:, 0:KV_W], kg_ref[...], lo)
        kb_ref[rows, :] = _pad_heads(_rope(zk, cos, sin_up, sin_dn), lo, 0.0)
        vb_ref[rows, :] = _pad_heads(z[:, KV_W:2 * KV_W], lo, 1.0)
        c += 2 * KV_W
        ga_ref[rows, :] = _sigmoid(proj(c, D_MODEL)).astype(_BF16)
        c += D_MODEL
        gb_ref[rows, :] = _sigmoid(proj(c, D_MODEL)).astype(_BF16)


def _in_proj(layer, x_lat, x_ctx, ctx_tile0, mod_l, g1, w_in, rope_tab, qg, kg):
    tile = lambda w: pl.BlockSpec((TM, w), lambda i: (i, 0))
    rope_spec = pl.BlockSpec(
        (TM, 3 * LANES), lambda i: (jnp.where(i < LAT_TILES, i % TILES_PER_SEQ, TILES_PER_SEQ), 0))
    widths = [Q_W, PAD_W, PAD_W, Q_W, PAD_W, PAD_W, D_MODEL, D_MODEL]
    return pl.pallas_call(
        _in_proj_kernel,
        grid=(ALL_TILES,),
        in_specs=[*_split_specs(D_MODEL, ctx_tile0), _mod_spec(), _resident((1, D_MODEL), layer),
                  _resident((D_MODEL, IN_COLS), layer), rope_spec,
                  _resident((1, LANES), layer), _resident((1, LANES), layer)],
        out_specs=[tile(w) for w in widths],
        out_shape=[jax.ShapeDtypeStruct((N_TOK, w), _BF16) for w in widths],
        compiler_params=_params(("arbitrary",)),
        name="in_proj",
    )(x_lat, x_ctx, mod_l, g1, w_in, rope_tab, qg, kg)


def _stack_heads(q, g):
    pairs = [q[:, (2 * g + t) * LANES:(2 * g + t + 1) * LANES] for t in range(2)]
    swapped = [pltpu.roll(p.astype(_F32), HEAD_DIM, 1).astype(_BF16) for p in pairs]
    return jnp.concatenate(pairs + swapped, axis=0)


def _store_heads(o_ref, g, o, extra_den=None):
    tq = o.shape[0] // HEADS_PER_KV
    lo = lax.broadcasted_iota(jnp.int32, (1, LANES), 1) < HEAD_DIM
    for t in range(2):
        a = slice(t * tq, (t + 1) * tq)
        b = slice((2 + t) * tq, (3 + t) * tq)
        same = jnp.where(lo, o[a], o[b])
        cross = pltpu.roll(jnp.where(lo, o[b], o[a]), HEAD_DIM, 1)
        if extra_den is None:
            pair = jnp.where(lo, same / cross, cross / same)
        else:
            ea, eb = extra_den[a], extra_den[b]
            pair = jnp.where(lo, same / (cross + ea), cross / (same + eb))
        o_ref[:, (2 * g + t) * LANES:(2 * g + t + 1) * LANES] = pair.astype(_BF16)


def _sink_column(sink_ref, g, tq):
    blk = lax.broadcasted_iota(jnp.int32, (HEADS_PER_KV * tq, 1), 0) // tq
    col = jnp.full((HEADS_PER_KV * tq, 1), sink_ref[HEADS_PER_KV * g + HEAD_ORDER[0]], _F32)
    for i in range(1, HEADS_PER_KV):
        col = jnp.where(blk == i, sink_ref[HEADS_PER_KV * g + HEAD_ORDER[i]], col)
    return col * LOG2E


def _fold_lanes(x, op):
    out = x[:, 0:LANES]
    for c in range(1, x.shape[1] // LANES):
        out = op(out, x[:, c * LANES:(c + 1) * LANES])
    return out


def _softmax_pv(scores, values, sink_col):
    s = scores[0] if len(scores) == 1 else jnp.concatenate(scores, axis=1)
    m = jnp.max(_fold_lanes(s, jnp.maximum), axis=-1, keepdims=True)
    if sink_col is not None:
        m = jnp.maximum(m, sink_col)
    pb = jnp.exp2(s - m).astype(_BF16)
    out, c0 = None, 0
    for v in values:
        o = jnp.dot(pb[:, c0:c0 + v.shape[0]], v, preferred_element_type=_F32)
        out = o if out is None else out + o
        c0 += v.shape[0]
    return out, (None if sink_col is None else jnp.exp2(sink_col - m))


def _window_kernel(sink_ref, q_ref, k_ref, v_ref, kc_ref, vc_ref, o_ref):
    j = pl.program_id(1)
    r = lax.broadcasted_iota(jnp.int32, (BLOCK, WIN_KEYS), 0)
    col = lax.broadcasted_iota(jnp.int32, (BLOCK, WIN_KEYS), 1)
    for blk in range(TQ_WINDOW // BLOCK):
        q0 = j * TQ_WINDOW + blk * BLOCK
        start = pl.multiple_of(jnp.clip(q0 - WINDOW, 0, SEQ - WIN_KEYS), BLOCK)
        bias = jnp.where(jnp.abs((start + col) - (q0 + r)) <= WINDOW, 0.0, NEG_INF)
        rows = slice(blk * BLOCK, (blk + 1) * BLOCK)
        q = q_ref[rows, :]
        for g in range(N_KV):
            qs = _stack_heads(q, g)
            cs = slice(g * LANES, (g + 1) * LANES)
            kw = k_ref[pl.ds(start, WIN_KEYS), cs]
            vw = v_ref[pl.ds(start, WIN_KEYS), cs]
            s_w = lax.dot_general(qs, kw, _NT, preferred_element_type=_F32)
            s_w = jnp.concatenate(
                [s_w[i * BLOCK:(i + 1) * BLOCK] + bias for i in range(HEADS_PER_KV)], axis=0)
            s_c = lax.dot_general(qs, kc_ref[:, cs], _NT, preferred_element_type=_F32)
            o, sink_term = _softmax_pv([s_w, s_c], [vw, vc_ref[:, cs]],
                                       _sink_column(sink_ref, g, BLOCK))
            _store_heads(o_ref.at[rows, :], g, o, sink_term)


def _window_attention(sink, qa, ka, va):
    nq = SEQ // TQ_WINDOW
    lat_kv = pl.BlockSpec((SEQ, PAD_W), lambda b, j: (b, 0))
    ctx_kv = pl.BlockSpec((CTX_LEN, PAD_W), lambda b, j: (N_LAT // CTX_LEN + b, 0))
    return pl.pallas_call(
        _window_kernel,
        grid=(BATCH, nq),
        in_specs=[pl.BlockSpec(memory_space=pltpu.SMEM),
                  pl.BlockSpec((TQ_WINDOW, Q_W), lambda b, j: (b * nq + j, 0)),
                  lat_kv, lat_kv, ctx_kv, ctx_kv],
        out_specs=pl.BlockSpec((TQ_WINDOW, Q_W), lambda b, j: (b * nq + j, 0)),
        out_shape=jax.ShapeDtypeStruct((N_LAT, Q_W), _BF16),
        compiler_params=_params(("arbitrary", "arbitrary")),
        name="window_attn",
    )(sink, qa, ka, va, ka, va)


def _dense_kernel(q_ref, k_ref, v_ref, kc_ref, vc_ref, o_ref, acc_ref, m_ref):
    qs = _stack_heads(q_ref[...], 0)
    m_ref[...] = jnp.full(m_ref.shape, -jnp.inf, _F32)
    acc_ref[...] = jnp.zeros(acc_ref.shape, _F32)

    def update(k, v):
        s = lax.dot_general(qs, k, _NT, preferred_element_type=_F32)
        m_prev = m_ref[...]
        m_new = jnp.maximum(m_prev, jnp.max(s, axis=-1, keepdims=True))
        p = jnp.exp2(s - m_new[:, 0:1])
        acc_ref[...] = (jnp.exp2(m_prev - m_new) * acc_ref[...]
                        + jnp.dot(p.astype(_BF16), v, preferred_element_type=_F32))
        m_ref[...] = m_new

    def chunk(j, carry):
        start = pl.multiple_of(j * TK_UNBOUNDED, TK_UNBOUNDED)
        update(k_ref[pl.ds(start, TK_UNBOUNDED), :], v_ref[pl.ds(start, TK_UNBOUNDED), :])
        return carry

    lax.fori_loop(0, SEQ // TK_UNBOUNDED, chunk, 0)
    update(kc_ref[...], vc_ref[...])
    _store_heads(o_ref, 0, acc_ref[...])


def _dense_bounded_kernel(q_ref, k_ref, v_ref, kc_ref, vc_ref, o_ref, acc_ref):
    qs = _stack_heads(q_ref[...], 0)

    def term(k, v):
        p = jnp.exp2(lax.dot_general(qs, k, _NT, preferred_element_type=_F32))
        return jnp.dot(p.astype(_BF16), v, preferred_element_type=_F32)

    acc_ref[...] = term(kc_ref[...], vc_ref[...])

    def chunk(j, carry):
        for u in range(TK_DENSE // TK_SUB):
            start = pl.multiple_of(j * TK_DENSE + u * TK_SUB, TK_SUB)
            acc_ref[...] += term(k_ref[pl.ds(start, TK_SUB), :], v_ref[pl.ds(start, TK_SUB), :])
        return carry

    lax.fori_loop(0, SEQ // TK_DENSE, chunk, 0)
    _store_heads(o_ref, 0, acc_ref[...])


def _dense_attention(qb, kb, vb, bounded):
    tq = TQ_DENSE if bounded else TQ_DENSE_UNBOUNDED
    nq = SEQ // tq
    lat_kv = pl.BlockSpec((SEQ, LANES), lambda b, g, i: (b, g))
    ctx_kv = pl.BlockSpec((CTX_LEN, LANES), lambda b, g, i: (N_LAT // CTX_LEN + b, g))
    stat = pltpu.VMEM((HEADS_PER_KV * tq, LANES), _F32)
    return pl.pallas_call(
        _dense_bounded_kernel if bounded else _dense_kernel,
        grid=(BATCH, N_KV, nq),
        in_specs=[pl.BlockSpec((tq, 2 * LANES), lambda b, g, i: (b * nq + i, g)),
                  lat_kv, lat_kv, ctx_kv, ctx_kv],
        out_specs=pl.BlockSpec((tq, 2 * LANES), lambda b, g, i: (b * nq + i, g)),
        out_shape=jax.ShapeDtypeStruct((N_LAT, Q_W), _BF16),
        scratch_shapes=[stat] if bounded else [stat, stat],
        compiler_params=_params(("arbitrary", "arbitrary", "arbitrary")),
        name="dense_attn_bounded" if bounded else "dense_attn",
    )(qb, kb, vb, kb, vb)


def _ctx_kernel(sink_ref, qa_ref, ka_ref, va_ref, qb_ref, kb_ref, vb_ref, oa_ref, ob_ref):
    for q_ref, k_ref, v_ref, o_ref, use_sink in ((qa_ref, ka_ref, va_ref, oa_ref, True),
                                                 (qb_ref, kb_ref, vb_ref, ob_ref, False)):
        for blk in range(CTX_LEN // BLOCK):
            rows = slice(blk * BLOCK, (blk + 1) * BLOCK)
            q = q_ref[rows, :]
            for g in range(N_KV):
                cs = slice(g * LANES, (g + 1) * LANES)
                s = lax.dot_general(_stack_heads(q, g), k_ref[:, cs], _NT,
                                    preferred_element_type=_F32)
                sink_col = _sink_column(sink_ref, g, BLOCK) if use_sink else None
                _store_heads(o_ref.at[rows, :], g, *_softmax_pv([s], [v_ref[:, cs]], sink_col))


def _ctx_attention(sink, qa, ka, va, qb, kb, vb):
    row = lambda w: pl.BlockSpec((CTX_LEN, w), lambda b: (N_LAT // CTX_LEN + b, 0))
    out = pl.BlockSpec((CTX_LEN, Q_W), lambda b: (b, 0))
    return pl.pallas_call(
        _ctx_kernel,
        grid=(BATCH,),
        in_specs=[pl.BlockSpec(memory_space=pltpu.SMEM),
                  row(Q_W), row(PAD_W), row(PAD_W), row(Q_W), row(PAD_W), row(PAD_W)],
        out_specs=[out, out],
        out_shape=[jax.ShapeDtypeStruct((N_CTX, Q_W), _BF16)] * 2,
        compiler_params=_params(("arbitrary",)),
        name="ctx_attn",
    )(sink, qa, ka, va, qb, kb, vb)


def _merge_kernel(xl_ref, xc_ref, mod_ref, yal_ref, yac_ref, ybl_ref, ybc_ref, ga_ref, gb_ref,
                  wa_ref, wb_ref, wo_ref, o_ref):
    gate = mod_ref[:, 2 * D_MODEL:3 * D_MODEL]
    for rows in _sub_tiles(SUB_MERGE):
        ya = _pick_stream(yal_ref, yac_ref, rows)
        yb = _pick_stream(ybl_ref, ybc_ref, rows)
        pa = jnp.dot(ya, wa_ref[...], preferred_element_type=_F32)
        pb = jnp.dot(yb, wb_ref[...], preferred_element_type=_F32)
        m = ga_ref[rows, :].astype(_F32) * pa + gb_ref[rows, :].astype(_F32) * pb
        y = jnp.dot(m.astype(_BF16), wo_ref[...], preferred_element_type=_F32)
        o_ref[rows, :] = _pick_stream(xl_ref, xc_ref, rows) + gate * y


def _merge(layer, x_lat, x_ctx, ctx_tile0, mod_l, ya, yac, yb, ybc, ga, gb, w_pa, w_pb, w_o, n_tiles):
    tile = lambda w: pl.BlockSpec((TM, w), lambda i: (i, 0))
    return pl.pallas_call(
        _merge_kernel,
        grid=(n_tiles,),
        in_specs=[*_split_specs(D_MODEL, ctx_tile0), _mod_spec(),
                  *_split_specs(Q_W, 0), *_split_specs(Q_W, 0), tile(D_MODEL), tile(D_MODEL),
                  _resident((Q_W, D_MODEL), layer), _resident((Q_W, D_MODEL), layer),
                  _resident((D_MODEL, D_MODEL), layer)],
        out_specs=tile(D_MODEL),
        out_shape=jax.ShapeDtypeStruct((n_tiles * TM, D_MODEL), _F32),
        compiler_params=_params(("arbitrary",)),
        name="merge",
    )(x_lat, x_ctx, mod_l, ya, yac, yb, ybc, ga, gb, w_pa, w_pb, w_o)


def _ffn_kernel(x_ref, mod_ref, g_ref, wg_ref, wu_ref, wd_ref, gf_ref, o_ref, *, final_norm):
    shift = mod_ref[:, 3 * D_MODEL:4 * D_MODEL]
    scale = mod_ref[:, 4 * D_MODEL:5 * D_MODEL]
    gate = mod_ref[:, 5 * D_MODEL:6 * D_MODEL]
    for rows in _sub_tiles(SUB_FFN):
        x = x_ref[rows, :]
        h = _norm_modulate(x, g_ref[...], shift, scale).astype(_BF16)
        a = jnp.dot(h, wg_ref[...], preferred_element_type=_F32)
        u = jnp.dot(h, wu_ref[...], preferred_element_type=_F32)
        act = ((a / (1.0 + jnp.exp(-a))) * u).astype(_BF16)
        y = x + gate * jnp.dot(act, wd_ref[...], preferred_element_type=_F32)
        if final_norm:
            ms = jnp.mean(y * y, axis=-1, keepdims=True)
            y = (y * lax.rsqrt(ms + NORM_EPS)) * gf_ref[...]
        o_ref[rows, :] = y


def _ffn(layer, x_all, mod_l, g2, w_gate, w_up, w_down, gf, n_tiles, final_norm):
    tile = pl.BlockSpec((TM, D_MODEL), lambda i: (i, 0))
    return pl.pallas_call(
        functools.partial(_ffn_kernel, final_norm=final_norm),
        grid=(n_tiles,),
        in_specs=[tile, _mod_spec(), _resident((1, D_MODEL), layer),
                  _resident((D_MODEL, D_FF), layer), _resident((D_MODEL, D_FF), layer),
                  _resident((D_FF, D_MODEL), layer), _resident((1, D_MODEL))],
        out_specs=tile,
        out_shape=jax.ShapeDtypeStruct((n_tiles * TM, D_MODEL), _F32),
        compiler_params=_params(("arbitrary",)),
        name="ffn",
    )(x_all, mod_l, g2, w_gate, w_up, w_down, gf)


def _rope_table():
    rows = SEQ // GRID_W
    row = jnp.repeat(jnp.arange(rows, dtype=_F32), GRID_W)
    col = jnp.tile(jnp.arange(GRID_W, dtype=_F32), rows)
    axis_dims = HEAD_DIM // 2
    inv = ROPE_THETA ** (-jnp.arange(0, axis_dims, 2, dtype=_F32) / axis_dims)
    ang = jnp.stack([row[:, None] * inv, col[:, None] * inv], axis=1)
    cos, sin = jnp.cos(ang), jnp.sin(ang)
    lane = np.arange(LANES)
    axis = (lane % HEAD_DIM) // axis_dims
    upper = ((lane % axis_dims) // (axis_dims // 2)) == 1
    j = lane % (axis_dims // 2)
    cos_l = cos[:, axis, j]
    sin_l = sin[:, axis, j]
    sin_up = jnp.where(upper[None, :], sin_l, 0.0)
    sin_dn = jnp.where(upper[None, :], 0.0, -sin_l)
    lat = jnp.concatenate([cos_l, sin_up, sin_dn], axis=1)
    ident = jnp.concatenate([jnp.ones((TM, LANES), _F32), jnp.zeros((TM, 2 * LANES), _F32)], axis=1)
    return jnp.concatenate([lat, ident], axis=0)


def kernel(x, c, ctx, c_ctx, w_ada, b_ada, norm1_g, norm2_g, w_in, q_norm_g, k_norm_g, sink_a,
           w_proj_a, w_proj_b, w_out, w_ffn_gate, w_ffn_up, w_ffn_down, final_norm_g):
    assert x.shape == (BATCH, SEQ, D_MODEL) and ctx.shape == (BATCH, CTX_LEN, D_MODEL)
    x_lat, x_ctx, ctx_tile0 = x.reshape(N_LAT, D_MODEL), ctx.reshape(N_CTX, D_MODEL), 0
    cc = jnp.concatenate([c, c_ctx[None, :], jnp.zeros((MOD_ROWS - BATCH - 1, D_MODEL), _F32)], axis=0)
    mod = _modulation(cc, w_ada, b_ada).reshape(DEPTH, MOD_ROWS, 1, 6 * D_MODEL)
    rope_tab = _rope_table()
    gf = final_norm_g.reshape(1, D_MODEL)
    w_in, w_proj_a, w_proj_b, w_out, w_ffn_gate, w_ffn_up, w_ffn_down = (
        w.astype(_BF16) for w in (w_in, w_proj_a, w_proj_b, w_out, w_ffn_gate, w_ffn_up, w_ffn_down))
    norm1_g = norm1_g.reshape(DEPTH, 1, D_MODEL)
    norm2_g = norm2_g.reshape(DEPTH, 1, D_MODEL)
    qg = jnp.tile(q_norm_g, (1, 2)).reshape(DEPTH, 1, LANES)
    kg = jnp.tile(k_norm_g, (1, 2)).reshape(DEPTH, 1, LANES)
    out = None
    for l in range(DEPTH):
        last = l == DEPTH - 1
        qa, ka, va, qb, kb, vb, ga, gb = _in_proj(
            l, x_lat, x_ctx, ctx_tile0, mod[l], norm1_g, w_in, rope_tab, qg, kg)
        ya = _window_attention(sink_a[l], qa, ka, va)
        score_bound = HEAD_DIM ** 0.5 * jnp.max(jnp.abs(q_norm_g[l])) * jnp.max(jnp.abs(k_norm_g[l]))
        yb = lax.cond(score_bound <= SCORE_BOUND,
                      functools.partial(_dense_attention, bounded=True),
                      functools.partial(_dense_attention, bounded=False), qb, kb, vb)
        n_tiles = LAT_TILES if last else ALL_TILES
        yac, ybc = (ya, yb) if last else _ctx_attention(sink_a[l], qa, ka, va, qb, kb, vb)
        x_mid = _merge(l, x_lat, x_ctx, ctx_tile0, mod[l], ya, yac, yb, ybc, ga, gb,
                       w_proj_a, w_proj_b, w_out, n_tiles)
        x_new = _ffn(l, x_mid, mod[l], norm2_g, w_ffn_gate, w_ffn_up, w_ffn_down, gf, n_tiles, last)
        if last:
            out = x_new
        else:
            x_lat, x_ctx, ctx_tile0 = x_new, x_new, LAT_TILES
    return out.reshape(BATCH, SEQ, D_MODEL)
```

```python
import functools

import jax
import jax.numpy as jnp
import numpy as np
from jax import lax
from jax.experimental import pallas as pl
from jax.experimental.pallas import tpu as pltpu

D_MODEL = 1024
BATCH = 4
SEQ = 4096
DEPTH = 4
CTX_LEN = 256
GRID_W = 64
HEAD_DIM = 64
N_HEADS = 8
N_KV = 2
WINDOW = 128
BLOCK = 128
D_FF = 2816
ROPE_THETA = 10000.0
NORM_EPS = 1e-6
NEG_INF = -1e30
Q_W = N_HEADS * HEAD_DIM
KV_W = N_KV * HEAD_DIM
IN_COLS = 2 * (Q_W + 2 * KV_W) + 2 * D_MODEL
LANES = 128
PAD_W = N_KV * LANES
HEADS_PER_KV = N_HEADS // N_KV
HEAD_ORDER = (0, 2, 1, 3)

N_LAT = BATCH * SEQ
N_CTX = BATCH * CTX_LEN
N_TOK = N_LAT + N_CTX
MOD_ROWS = 8
CTX_MOD_ROW = BATCH

TM = 1024
SUB_IN_PROJ, SUB_MERGE, SUB_FFN = 256, 512, 256
LAT_TILES = N_LAT // TM
ALL_TILES = N_TOK // TM
TILES_PER_SEQ = SEQ // TM
TQ_DENSE = 1024
TQ_DENSE_UNBOUNDED = 128
TQ_WINDOW = 1024
WIN_KEYS = BLOCK + 2 * WINDOW
TK_DENSE = 2048
TK_UNBOUNDED = 1024
TK_SUB = 256
LOG2E = 1.4426950408889634
SCORE_BOUND = 60.0
LOGIT_BOUND = 80.0
VMEM_LIMIT = 56 * 1024 * 1024

_F32 = jnp.float32
_BF16 = jnp.bfloat16
_NT = (((1,), (1,)), ((), ()))


def _params(sem):
    return pltpu.CompilerParams(dimension_semantics=sem, vmem_limit_bytes=VMEM_LIMIT)


def _resident(shape, layer=None):
    nd = len(shape)
    if layer is None:
        return pl.BlockSpec(shape, lambda *_: (0,) * nd, pipeline_mode=pl.Buffered(1))
    return pl.BlockSpec((None,) + tuple(shape), lambda *_: (layer,) + (0,) * nd,
                        pipeline_mode=pl.Buffered(1))


def _mod_row_of_tile(i):
    return jnp.minimum(i // TILES_PER_SEQ, CTX_MOD_ROW)


def _mod_spec():
    return pl.BlockSpec((None, 1, 6 * D_MODEL), lambda i: (_mod_row_of_tile(i), 0, 0))


def _split_specs(width, ctx_tile0):
    lat = pl.BlockSpec((TM, width), lambda i: (jnp.minimum(i, LAT_TILES - 1), 0))
    ctx = pl.BlockSpec((TM, width), lambda i: (jnp.maximum(i - LAT_TILES, 0) + ctx_tile0, 0))
    return lat, ctx


def _pick_stream(lat_ref, ctx_ref, rows):
    return jnp.where(pl.program_id(0) < LAT_TILES, lat_ref[rows, :], ctx_ref[rows, :])


def _sub_tiles(sub_m):
    return [slice(s * sub_m, (s + 1) * sub_m) for s in range(TM // sub_m)]


def _mod_kernel(c_ref, w_ref, b_ref, o_ref):
    c = c_ref[...]
    a = c / (1.0 + jnp.exp(-c))
    o_ref[...] = jnp.dot(a, w_ref[...], preferred_element_type=_F32) + b_ref[...]


def _modulation(cc, w_ada, b_ada):
    tn = 1536
    return pl.pallas_call(
        _mod_kernel,
        grid=(DEPTH, 6 * D_MODEL // tn),
        in_specs=[
            pl.BlockSpec((MOD_ROWS, D_MODEL), lambda l, j: (0, 0)),
            pl.BlockSpec((None, D_MODEL, tn), lambda l, j: (l, 0, j)),
            pl.BlockSpec((None, 1, tn), lambda l, j: (l, 0, j)),
        ],
        out_specs=pl.BlockSpec((None, MOD_ROWS, tn), lambda l, j: (l, 0, j)),
        out_shape=jax.ShapeDtypeStruct((DEPTH, MOD_ROWS, 6 * D_MODEL), _F32),
        compiler_params=_params(("arbitrary", "arbitrary")),
        name="adaln_mod",
    )(cc, w_ada, b_ada.reshape(DEPTH, 1, 6 * D_MODEL))


def _norm_modulate(x, g, shift, scale):
    ms = jnp.mean(x * x, axis=-1, keepdims=True)
    return (x * lax.rsqrt(ms + NORM_EPS)) * (g * (1.0 + scale)) + shift


def _rope(z, cos, sin_up, sin_dn):
    return z * cos + pltpu.roll(z, 16, 1) * sin_up + pltpu.roll(z, LANES - 16, 1) * sin_dn


def _head_rms(z, gain, lo):
    z2 = z * z
    s_all = jnp.sum(z2, axis=-1, keepdims=True)
    s_lo = jnp.sum(jnp.where(lo, z2, 0.0), axis=-1, keepdims=True)
    ms = jnp.where(lo, s_lo, s_all - s_lo) * (1.0 / HEAD_DIM)
    return (z * lax.rsqrt(ms + NORM_EPS)) * gain


def _pad_heads(z, lo, fill):
    sw = pltpu.roll(z, HEAD_DIM, 1)
    return jnp.concatenate([jnp.where(lo, z, fill), jnp.where(lo, sw, fill)], axis=1).astype(_BF16)


def _sigmoid(z):
    return 1.0 / (1.0 + jnp.exp(-z))


def _in_proj_kernel(xl_ref, xc_ref, mod_ref, g_ref, w_ref, rope_ref, qg_ref, kg_ref,
                    qa_ref, ka_ref, va_ref, qb_ref, kb_ref, vb_ref, ga_ref, gb_ref):
    shift = mod_ref[:, 0:D_MODEL]
    scale = mod_ref[:, D_MODEL:2 * D_MODEL]
    lo = lax.broadcasted_iota(jnp.int32, (1, LANES), 1) < HEAD_DIM
    qscale = HEAD_DIM ** -0.5 * LOG2E
    qg = qg_ref[...]
    for rows in _sub_tiles(SUB_IN_PROJ):
        x = _pick_stream(xl_ref, xc_ref, rows)
        h = _norm_modulate(x, g_ref[...], shift, scale).astype(_BF16)
        cos = rope_ref[rows, 0:LANES]
        sin_up = rope_ref[rows, LANES:2 * LANES]
        sin_dn = rope_ref[rows, 2 * LANES:3 * LANES]

        def proj(c0, width):
            return jnp.dot(h, w_ref[:, c0:c0 + width], preferred_element_type=_F32)

        c = 0
        z = proj(c, Q_W)
        for j in range(Q_W // LANES):
            zj = _rope(z[:, j * LANES:(j + 1) * LANES], cos, sin_up, sin_dn)
            qa_ref[rows, j * LANES:(j + 1) * LANES] = (zj * qscale).astype(_BF16)
        c += Q_W
        z = proj(c, 2 * KV_W)
        ka_ref[rows, :] = _pad_heads(_rope(z[:, 0:KV_W], cos, sin_up, sin_dn), lo, 0.0)
        va_ref[rows, :] = _pad_heads(z[:, KV_W:2 * KV_W], lo, 1.0)
        c += 2 * KV_W
        z = proj(c, Q_W)
        for j in range(Q_W // LANES):
            zj = _rope(_head_rms(z[:, j * LANES:(j + 1) * LANES], qg, lo), cos, sin_up, sin_dn)
            qb_ref[rows, j * LANES:(j + 1) * LANES] = (zj * qscale).astype(_BF16)
        c += Q_W
        z = proj(c, 2 * KV_W)
        zk = _head_rms(z[:, 0:KV_W], kg_ref[...], lo)
        kb_ref[rows, :] = _pad_heads(_rope(zk, cos, sin_up, sin_dn), lo, 0.0)
        vb_ref[rows, :] = _pad_heads(z[:, KV_W:2 * KV_W], lo, 1.0)
        c += 2 * KV_W
        ga_ref[rows, :] = _sigmoid(proj(c, D_MODEL)).astype(_BF16)
        c += D_MODEL
        gb_ref[rows, :] = _sigmoid(proj(c, D_MODEL)).astype(_BF16)


def _in_proj(layer, x_lat, x_ctx, ctx_tile0, mod_l, g1, w_in, rope_tab, qg, kg):
    tile = lambda w: pl.BlockSpec((TM, w), lambda i: (i, 0))
    rope_spec = pl.BlockSpec(
        (TM, 3 * LANES), lambda i: (jnp.where(i < LAT_TILES, i % TILES_PER_SEQ, TILES_PER_SEQ), 0))
    widths = [Q_W, PAD_W, PAD_W, Q_W, PAD_W, PAD_W, D_MODEL, D_MODEL]
    return pl.pallas_call(
        _in_proj_kernel,
        grid=(ALL_TILES,),
        in_specs=[*_split_specs(D_MODEL, ctx_tile0), _mod_spec(), _resident((1, D_MODEL), layer),
                  _resident((D_MODEL, IN_COLS), layer), rope_spec,
                  _resident((1, LANES), layer), _resident((1, LANES), layer)],
        out_specs=[tile(w) for w in widths],
        out_shape=[jax.ShapeDtypeStruct((N_TOK, w), _BF16) for w in widths],
        compiler_params=_params(("arbitrary",)),
        name="in_proj",
    )(x_lat, x_ctx, mod_l, g1, w_in, rope_tab, qg, kg)


def _stack_heads(q, g):
    pairs = [q[:, (2 * g + t) * LANES:(2 * g + t + 1) * LANES] for t in range(2)]
    swapped = [pltpu.roll(p.astype(_F32), HEAD_DIM, 1).astype(_BF16) for p in pairs]
    return jnp.concatenate(pairs + swapped, axis=0)


def _store_heads(o_ref, g, o, extra_den=None):
    tq = o.shape[0] // HEADS_PER_KV
    lo = lax.broadcasted_iota(jnp.int32, (1, LANES), 1) < HEAD_DIM
    for t in range(2):
        a = slice(t * tq, (t + 1) * tq)
        b = slice((2 + t) * tq, (3 + t) * tq)
        same = jnp.where(lo, o[a], o[b])
        cross = pltpu.roll(jnp.where(lo, o[b], o[a]), HEAD_DIM, 1)
        if extra_den is None:
            pair = jnp.where(lo, same / cross, cross / same)
        else:
            ea, eb = extra_den[a], extra_den[b]
            pair = jnp.where(lo, same / (cross + ea), cross / (same + eb))
        o_ref[:, (2 * g + t) * LANES:(2 * g + t + 1) * LANES] = pair.astype(_BF16)


def _sink_column(sink_ref, g, tq):
    blk = lax.broadcasted_iota(jnp.int32, (HEADS_PER_KV * tq, 1), 0) // tq
    col = jnp.full((HEADS_PER_KV * tq, 1), sink_ref[HEADS_PER_KV * g + HEAD_ORDER[0]], _F32)
    for i in range(1, HEADS_PER_KV):
        col = jnp.where(blk == i, sink_ref[HEADS_PER_KV * g + HEAD_ORDER[i]], col)
    return col * LOG2E


def _fold_lanes(x, op):
    out = x[:, 0:LANES]
    for c in range(1, x.shape[1] // LANES):
        out = op(out, x[:, c * LANES:(c + 1) * LANES])
    return out


def _softmax_pv(scores, values, sink_col, bounded=False):
    if bounded:
        parts = [jnp.exp2(s).astype(_BF16) for s in scores]
        sink_term = None if sink_col is None else jnp.exp2(sink_col)
    else:
        s = scores[0] if len(scores) == 1 else jnp.concatenate(scores, axis=1)
        m = jnp.max(_fold_lanes(s, jnp.maximum), axis=-1, keepdims=True)
        if sink_col is not None:
            m = jnp.maximum(m, sink_col)
        pb = jnp.exp2(s - m).astype(_BF16)
        parts, c0 = [], 0
        for v in values:
            parts.append(pb[:, c0:c0 + v.shape[0]])
            c0 += v.shape[0]
        sink_term = None if sink_col is None else jnp.exp2(sink_col - m)
    out = None
    for p, v in zip(parts, values):
        o = jnp.dot(p, v, preferred_element_type=_F32)
        out = o if out is None else out + o
    return out, sink_term


def _window_bounded_kernel(sink_ref, q_ref, k_ref, v_ref, kc_ref, vc_ref, o_ref):
    j = pl.program_id(1)
    nblk = TQ_WINDOW // BLOCK
    r = lax.broadcasted_iota(jnp.int32, (BLOCK, WIN_KEYS), 0)
    col = lax.broadcasted_iota(jnp.int32, (BLOCK, WIN_KEYS), 1)
    q = q_ref[...]
    for g in range(N_KV):
        cs = slice(g * LANES, (g + 1) * LANES)
        qs = _stack_heads(q, g)
        p_c = jnp.exp2(lax.dot_general(qs, kc_ref[:, cs], _NT, preferred_element_type=_F32))
        o_c = jnp.dot(p_c.astype(_BF16), vc_ref[:, cs], preferred_element_type=_F32)
        sink_term = jnp.exp2(_sink_column(sink_ref, g, BLOCK))
        for blk in range(nblk):
            q0 = j * TQ_WINDOW + blk * BLOCK
            start = pl.multiple_of(jnp.clip(q0 - WINDOW, 0, SEQ - WIN_KEYS), BLOCK)
            bias = jnp.where(jnp.abs((start + col) - (q0 + r)) <= WINDOW, 0.0, NEG_INF)
            rows = [slice(i * TQ_WINDOW + blk * BLOCK, i * TQ_WINDOW + (blk + 1) * BLOCK)
                    for i in range(HEADS_PER_KV)]
            s_w = lax.dot_general(jnp.concatenate([qs[rw] for rw in rows], axis=0),
                                  k_ref[pl.ds(start, WIN_KEYS), cs], _NT,
                                  preferred_element_type=_F32)
            p_w = jnp.concatenate(
                [jnp.exp2(s_w[i * BLOCK:(i + 1) * BLOCK] + bias) for i in range(HEADS_PER_KV)], axis=0)
            o = (jnp.dot(p_w.astype(_BF16), v_ref[pl.ds(start, WIN_KEYS), cs],
                         preferred_element_type=_F32)
                 + jnp.concatenate([o_c[rw] for rw in rows], axis=0))
            _store_heads(o_ref.at[blk * BLOCK:(blk + 1) * BLOCK, :], g, o, sink_term)


def _window_kernel(sink_ref, q_ref, k_ref, v_ref, kc_ref, vc_ref, o_ref, *, bounded):
    j = pl.program_id(1)
    r = lax.broadcasted_iota(jnp.int32, (BLOCK, WIN_KEYS), 0)
    col = lax.broadcasted_iota(jnp.int32, (BLOCK, WIN_KEYS), 1)
    for blk in range(TQ_WINDOW // BLOCK):
        q0 = j * TQ_WINDOW + blk * BLOCK
        start = pl.multiple_of(jnp.clip(q0 - WINDOW, 0, SEQ - WIN_KEYS), BLOCK)
        bias = jnp.where(jnp.abs((start + col) - (q0 + r)) <= WINDOW, 0.0, NEG_INF)
        rows = slice(blk * BLOCK, (blk + 1) * BLOCK)
        q = q_ref[rows, :]
        for g in range(N_KV):
            qs = _stack_heads(q, g)
            cs = slice(g * LANES, (g + 1) * LANES)
            kw = k_ref[pl.ds(start, WIN_KEYS), cs]
            vw = v_ref[pl.ds(start, WIN_KEYS), cs]
            s_w = lax.dot_general(qs, kw, _NT, preferred_element_type=_F32)
            s_w = jnp.concatenate(
                [s_w[i * BLOCK:(i + 1) * BLOCK] + bias for i in range(HEADS_PER_KV)], axis=0)
            s_c = lax.dot_general(qs, kc_ref[:, cs], _NT, preferred_element_type=_F32)
            o, sink_term = _softmax_pv([s_w, s_c], [vw, vc_ref[:, cs]],
                                       _sink_column(sink_ref, g, BLOCK), bounded)
            _store_heads(o_ref.at[rows, :], g, o, sink_term)


def _window_attention(sink, qa, ka, va, bounded):
    nq = SEQ // TQ_WINDOW
    lat_kv = pl.BlockSpec((SEQ, PAD_W), lambda b, j: (b, 0))
    ctx_kv = pl.BlockSpec((CTX_LEN, PAD_W), lambda b, j: (N_LAT // CTX_LEN + b, 0))
    return pl.pallas_call(
        _window_bounded_kernel if bounded else functools.partial(_window_kernel, bounded=False),
        grid=(BATCH, nq),
        in_specs=[pl.BlockSpec(memory_space=pltpu.SMEM),
                  pl.BlockSpec((TQ_WINDOW, Q_W), lambda b, j: (b * nq + j, 0)),
                  lat_kv, lat_kv, ctx_kv, ctx_kv],
        out_specs=pl.BlockSpec((TQ_WINDOW, Q_W), lambda b, j: (b * nq + j, 0)),
        out_shape=jax.ShapeDtypeStruct((N_LAT, Q_W), _BF16),
        compiler_params=_params(("arbitrary", "arbitrary")),
        name="window_attn_bounded" if bounded else "window_attn",
    )(sink, qa, ka, va, ka, va)


def _dense_kernel(q_ref, k_ref, v_ref, kc_ref, vc_ref, o_ref, acc_ref, m_ref):
    qs = _stack_heads(q_ref[...], 0)
    m_ref[...] = jnp.full(m_ref.shape, -jnp.inf, _F32)
    acc_ref[...] = jnp.zeros(acc_ref.shape, _F32)

    def update(k, v):
        s = lax.dot_general(qs, k, _NT, preferred_element_type=_F32)
        m_prev = m_ref[...]
        m_new = jnp.maximum(m_prev, jnp.max(s, axis=-1, keepdims=True))
        p = jnp.exp2(s - m_new[:, 0:1])
        acc_ref[...] = (jnp.exp2(m_prev - m_new) * acc_ref[...]
                        + jnp.dot(p.astype(_BF16), v, preferred_element_type=_F32))
        m_ref[...] = m_new

    def chunk(j, carry):
        start = pl.multiple_of(j * TK_UNBOUNDED, TK_UNBOUNDED)
        update(k_ref[pl.ds(start, TK_UNBOUNDED), :], v_ref[pl.ds(start, TK_UNBOUNDED), :])
        return carry

    lax.fori_loop(0, SEQ // TK_UNBOUNDED, chunk, 0)
    update(kc_ref[...], vc_ref[...])
    _store_heads(o_ref, 0, acc_ref[...])


def _dense_bounded_kernel(q_ref, k_ref, v_ref, kc_ref, vc_ref, o_ref, acc_ref):
    qs = _stack_heads(q_ref[...], 0)

    def term(k, v):
        p = jnp.exp2(lax.dot_general(qs, k, _NT, preferred_element_type=_F32))
        return jnp.dot(p.astype(_BF16), v, preferred_element_type=_F32)

    acc_ref[...] = term(kc_ref[...], vc_ref[...])

    def chunk(j, carry):
        for u in range(TK_DENSE // TK_SUB):
            start = pl.multiple_of(j * TK_DENSE + u * TK_SUB, TK_SUB)
            acc_ref[...] += term(k_ref[pl.ds(start, TK_SUB), :], v_ref[pl.ds(start, TK_SUB), :])
        return carry

    lax.fori_loop(0, SEQ // TK_DENSE, chunk, 0)
    _store_heads(o_ref, 0, acc_ref[...])


def _dense_attention(qb, kb, vb, bounded):
    tq = TQ_DENSE if bounded else TQ_DENSE_UNBOUNDED
    nq = SEQ // tq
    lat_kv = pl.BlockSpec((SEQ, LANES), lambda b, g, i: (b, g))
    ctx_kv = pl.BlockSpec((CTX_LEN, LANES), lambda b, g, i: (N_LAT // CTX_LEN + b, g))
    stat = pltpu.VMEM((HEADS_PER_KV * tq, LANES), _F32)
    return pl.pallas_call(
        _dense_bounded_kernel if bounded else _dense_kernel,
        grid=(BATCH, N_KV, nq),
        in_specs=[pl.BlockSpec((tq, 2 * LANES), lambda b, g, i: (b * nq + i, g)),
                  lat_kv, lat_kv, ctx_kv, ctx_kv],
        out_specs=pl.BlockSpec((tq, 2 * LANES), lambda b, g, i: (b * nq + i, g)),
        out_shape=jax.ShapeDtypeStruct((N_LAT, Q_W), _BF16),
        scratch_shapes=[stat] if bounded else [stat, stat],
        compiler_params=_params(("arbitrary", "arbitrary", "arbitrary")),
        name="dense_attn_bounded" if bounded else "dense_attn",
    )(qb, kb, vb, kb, vb)


def _ctx_kernel(sink_ref, qa_ref, ka_ref, va_ref, qb_ref, kb_ref, vb_ref, oa_ref, ob_ref):
    for q_ref, k_ref, v_ref, o_ref, use_sink in ((qa_ref, ka_ref, va_ref, oa_ref, True),
                                                 (qb_ref, kb_ref, vb_ref, ob_ref, False)):
        for blk in range(CTX_LEN // BLOCK):
            rows = slice(blk * BLOCK, (blk + 1) * BLOCK)
            q = q_ref[rows, :]
            for g in range(N_KV):
                cs = slice(g * LANES, (g + 1) * LANES)
                s = lax.dot_general(_stack_heads(q, g), k_ref[:, cs], _NT,
                                    preferred_element_type=_F32)
                sink_col = _sink_column(sink_ref, g, BLOCK) if use_sink else None
                _store_heads(o_ref.at[rows, :], g, *_softmax_pv([s], [v_ref[:, cs]], sink_col))


def _ctx_attention(sink, qa, ka, va, qb, kb, vb):
    row = lambda w: pl.BlockSpec((CTX_LEN, w), lambda b: (N_LAT // CTX_LEN + b, 0))
    out = pl.BlockSpec((CTX_LEN, Q_W), lambda b: (b, 0))
    return pl.pallas_call(
        _ctx_kernel,
        grid=(BATCH,),
        in_specs=[pl.BlockSpec(memory_space=pltpu.SMEM),
                  row(Q_W), row(PAD_W), row(PAD_W), row(Q_W), row(PAD_W), row(PAD_W)],
        out_specs=[out, out],
        out_shape=[jax.ShapeDtypeStruct((N_CTX, Q_W), _BF16)] * 2,
        compiler_params=_params(("arbitrary",)),
        name="ctx_attn",
    )(sink, qa, ka, va, qb, kb, vb)


def _merge_kernel(xl_ref, xc_ref, mod_ref, yal_ref, yac_ref, ybl_ref, ybc_ref, ga_ref, gb_ref,
                  wa_ref, wb_ref, wo_ref, o_ref):
    gate = mod_ref[:, 2 * D_MODEL:3 * D_MODEL]
    for rows in _sub_tiles(SUB_MERGE):
        ya = _pick_stream(yal_ref, yac_ref, rows)
        yb = _pick_stream(ybl_ref, ybc_ref, rows)
        pa = jnp.dot(ya, wa_ref[...], preferred_element_type=_F32)
        pb = jnp.dot(yb, wb_ref[...], preferred_element_type=_F32)
        m = ga_ref[rows, :].astype(_F32) * pa + gb_ref[rows, :].astype(_F32) * pb
        y = jnp.dot(m.astype(_BF16), wo_ref[...], preferred_element_type=_F32)
        o_ref[rows, :] = _pick_stream(xl_ref, xc_ref, rows) + gate * y


def _merge(layer, x_lat, x_ctx, ctx_tile0, mod_l, ya, yac, yb, ybc, ga, gb, w_pa, w_pb, w_o, n_tiles):
    tile = lambda w: pl.BlockSpec((TM, w), lambda i: (i, 0))
    return pl.pallas_call(
        _merge_kernel,
        grid=(n_tiles,),
        in_specs=[*_split_specs(D_MODEL, ctx_tile0), _mod_spec(),
                  *_split_specs(Q_W, 0), *_split_specs(Q_W, 0), tile(D_MODEL), tile(D_MODEL),
                  _resident((Q_W, D_MODEL), layer), _resident((Q_W, D_MODEL), layer),
                  _resident((D_MODEL, D_MODEL), layer)],
        out_specs=tile(D_MODEL),
        out_shape=jax.ShapeDtypeStruct((n_tiles * TM, D_MODEL), _F32),
        compiler_params=_params(("arbitrary",)),
        name="merge",
    )(x_lat, x_ctx, mod_l, ya, yac, yb, ybc, ga, gb, w_pa, w_pb, w_o)


def _ffn_kernel(x_ref, mod_ref, g_ref, wg_ref, wu_ref, wd_ref, gf_ref, o_ref, *, final_norm):
    shift = mod_ref[:, 3 * D_MODEL:4 * D_MODEL]
    scale = mod_ref[:, 4 * D_MODEL:5 * D_MODEL]
    gate = mod_ref[:, 5 * D_MODEL:6 * D_MODEL]
    for rows in _sub_tiles(SUB_FFN):
        x = x_ref[rows, :]
        h = _norm_modulate(x, g_ref[...], shift, scale).astype(_BF16)
        a = jnp.dot(h, wg_ref[...], preferred_element_type=_F32)
        u = jnp.dot(h, wu_ref[...], preferred_element_type=_F32)
        act = ((a / (1.0 + jnp.exp(-a))) * u).astype(_BF16)
        y = x + gate * jnp.dot(act, wd_ref[...], preferred_element_type=_F32)
        if final_norm:
            ms = jnp.mean(y * y, axis=-1, keepdims=True)
            y = (y * lax.rsqrt(ms + NORM_EPS)) * gf_ref[...]
        o_ref[rows, :] = y


def _ffn(layer, x_all, mod_l, g2, w_gate, w_up, w_down, gf, n_tiles, final_norm):
    tile = pl.BlockSpec((TM, D_MODEL), lambda i: (i, 0))
    return pl.pallas_call(
        functools.partial(_ffn_kernel, final_norm=final_norm),
        grid=(n_tiles,),
        in_specs=[tile, _mod_spec(), _resident((1, D_MODEL), layer),
                  _resident((D_MODEL, D_FF), layer), _resident((D_MODEL, D_FF), layer),
                  _resident((D_FF, D_MODEL), layer), _resident((1, D_MODEL))],
        out_specs=tile,
        out_shape=jax.ShapeDtypeStruct((n_tiles * TM, D_MODEL), _F32),
        compiler_params=_params(("arbitrary",)),
        name="ffn",
    )(x_all, mod_l, g2, w_gate, w_up, w_down, gf)


def _rope_table():
    rows = SEQ // GRID_W
    row = jnp.repeat(jnp.arange(rows, dtype=_F32), GRID_W)
    col = jnp.tile(jnp.arange(GRID_W, dtype=_F32), rows)
    axis_dims = HEAD_DIM // 2
    inv = ROPE_THETA ** (-jnp.arange(0, axis_dims, 2, dtype=_F32) / axis_dims)
    ang = jnp.stack([row[:, None] * inv, col[:, None] * inv], axis=1)
    cos, sin = jnp.cos(ang), jnp.sin(ang)
    lane = np.arange(LANES)
    axis = (lane % HEAD_DIM) // axis_dims
    upper = ((lane % axis_dims) // (axis_dims // 2)) == 1
    j = lane % (axis_dims // 2)
    cos_l = cos[:, axis, j]
    sin_l = sin[:, axis, j]
    sin_up = jnp.where(upper[None, :], sin_l, 0.0)
    sin_dn = jnp.where(upper[None, :], 0.0, -sin_l)
    lat = jnp.concatenate([cos_l, sin_up, sin_dn], axis=1)
    ident = jnp.concatenate([jnp.ones((TM, LANES), _F32), jnp.zeros((TM, 2 * LANES), _F32)], axis=1)
    return jnp.concatenate([lat, ident], axis=0)


def _window_logit_bound(qa, ka, sink):
    q2 = jnp.sum(jnp.square(qa[:N_LAT].astype(_F32)).reshape(N_LAT, N_HEADS, HEAD_DIM), axis=-1)
    k2 = jnp.sum(jnp.square(ka.astype(_F32)).reshape(N_TOK, N_KV, LANES), axis=-1)
    return jnp.maximum(jnp.sqrt(jnp.max(q2) * jnp.max(k2)), jnp.max(jnp.abs(sink)) * LOG2E)


def kernel(x, c, ctx, c_ctx, w_ada, b_ada, norm1_g, norm2_g, w_in, q_norm_g, k_norm_g, sink_a,
           w_proj_a, w_proj_b, w_out, w_ffn_gate, w_ffn_up, w_ffn_down, final_norm_g):
    assert x.shape == (BATCH, SEQ, D_MODEL) and ctx.shape == (BATCH, CTX_LEN, D_MODEL)
    x_lat, x_ctx, ctx_tile0 = x.reshape(N_LAT, D_MODEL), ctx.reshape(N_CTX, D_MODEL), 0
    cc = jnp.concatenate([c, c_ctx[None, :], jnp.zeros((MOD_ROWS - BATCH - 1, D_MODEL), _F32)], axis=0)
    mod = _modulation(cc, w_ada, b_ada).reshape(DEPTH, MOD_ROWS, 1, 6 * D_MODEL)
    rope_tab = _rope_table()
    gf = final_norm_g.reshape(1, D_MODEL)
    w_in, w_proj_a, w_proj_b, w_out, w_ffn_gate, w_ffn_up, w_ffn_down = (
        w.astype(_BF16) for w in (w_in, w_proj_a, w_proj_b, w_out, w_ffn_gate, w_ffn_up, w_ffn_down))
    norm1_g = norm1_g.reshape(DEPTH, 1, D_MODEL)
    norm2_g = norm2_g.reshape(DEPTH, 1, D_MODEL)
    qg = jnp.tile(q_norm_g, (1, 2)).reshape(DEPTH, 1, LANES)
    kg = jnp.tile(k_norm_g, (1, 2)).reshape(DEPTH, 1, LANES)
    out = None
    for l in range(DEPTH):
        last = l == DEPTH - 1
        qa, ka, va, qb, kb, vb, ga, gb = _in_proj(
            l, x_lat, x_ctx, ctx_tile0, mod[l], norm1_g, w_in, rope_tab, qg, kg)
        ya = lax.cond(_window_logit_bound(qa, ka, sink_a[l]) <= LOGIT_BOUND,
                      functools.partial(_window_attention, bounded=True),
                      functools.partial(_window_attention, bounded=False), sink_a[l], qa, ka, va)
        score_bound = HEAD_DIM ** 0.5 * jnp.max(jnp.abs(q_norm_g[l])) * jnp.max(jnp.abs(k_norm_g[l]))
        yb = lax.cond(score_bound <= SCORE_BOUND,
                      functools.partial(_dense_attention, bounded=True),
                      functools.partial(_dense_attention, bounded=False), qb, kb, vb)
        n_tiles = LAT_TILES if last else ALL_TILES
        yac, ybc = (ya, yb) if last else _ctx_attention(sink_a[l], qa, ka, va, qb, kb, vb)
        x_mid = _merge(l, x_lat, x_ctx, ctx_tile0, mod[l], ya, yac, yb, ybc, ga, gb,
                       w_proj_a, w_proj_b, w_out, n_tiles)
        x_new = _ffn(l, x_mid, mod[l], norm2_g, w_ffn_gate, w_ffn_up, w_ffn_down, gf, n_tiles, last)
        if last:
            out = x_new
        else:
            x_lat, x_ctx, ctx_tile0 = x_new, x_new, LAT_TILES
    return out.reshape(BATCH, SEQ, D_MODEL)
```

```python
import functools

import jax
import jax.numpy as jnp
import numpy as np
from jax import lax
from jax.experimental import pallas as pl
from jax.experimental.pallas import tpu as pltpu

D_MODEL = 1024
BATCH = 4
SEQ = 4096
DEPTH = 4
CTX_LEN = 256
GRID_W = 64
HEAD_DIM = 64
N_HEADS = 8
N_KV = 2
WINDOW = 128
BLOCK = 128
D_FF = 2816
ROPE_THETA = 10000.0
NORM_EPS = 1e-6
NEG_INF = -1e30
Q_W = N_HEADS * HEAD_DIM
KV_W = N_KV * HEAD_DIM
IN_COLS = 2 * (Q_W + 2 * KV_W) + 2 * D_MODEL
LANES = 128
PAD_W = N_KV * LANES
HEADS_PER_KV = N_HEADS // N_KV
HEAD_ORDER = (0, 2, 1, 3)

N_LAT = BATCH * SEQ
N_CTX = BATCH * CTX_LEN
N_TOK = N_LAT + N_CTX
MOD_ROWS = 8
CTX_MOD_ROW = BATCH

TM = 1024
SUB_IN_PROJ, SUB_MERGE, SUB_FFN = 256, 512, 256
LAT_TILES = N_LAT // TM
ALL_TILES = N_TOK // TM
TILES_PER_SEQ = SEQ // TM
TQ_DENSE = 1024
TQ_DENSE_UNBOUNDED = 128
TQ_WINDOW = 1024
WIN_KEYS = BLOCK + 2 * WINDOW
TK_DENSE = 2048
TK_UNBOUNDED = 1024
TK_SUB = 256
LOG2E = 1.4426950408889634
SCORE_BOUND = 60.0
LOGIT_BOUND = 80.0
VMEM_LIMIT = 56 * 1024 * 1024

_F32 = jnp.float32
_BF16 = jnp.bfloat16
_NT = (((1,), (1,)), ((), ()))


def _params(sem):
    return pltpu.CompilerParams(dimension_semantics=sem, vmem_limit_bytes=VMEM_LIMIT)


def _resident(shape, layer=None):
    nd = len(shape)
    if layer is None:
        return pl.BlockSpec(shape, lambda *_: (0,) * nd, pipeline_mode=pl.Buffered(1))
    return pl.BlockSpec((None,) + tuple(shape), lambda *_: (layer,) + (0,) * nd,
                        pipeline_mode=pl.Buffered(1))


def _mod_row_of_tile(i):
    return jnp.minimum(i // TILES_PER_SEQ, CTX_MOD_ROW)


def _mod_spec():
    return pl.BlockSpec((None, 1, 6 * D_MODEL), lambda i: (_mod_row_of_tile(i), 0, 0))


def _split_specs(width, ctx_tile0):
    lat = pl.BlockSpec((TM, width), lambda i: (jnp.minimum(i, LAT_TILES - 1), 0))
    ctx = pl.BlockSpec((TM, width), lambda i: (jnp.maximum(i - LAT_TILES, 0) + ctx_tile0, 0))
    return lat, ctx


def _pick_stream(lat_ref, ctx_ref, rows):
    return jnp.where(pl.program_id(0) < LAT_TILES, lat_ref[rows, :], ctx_ref[rows, :])


def _sub_tiles(sub_m):
    return [slice(s * sub_m, (s + 1) * sub_m) for s in range(TM // sub_m)]


def _mod_kernel(c_ref, w_ref, b_ref, o_ref):
    c = c_ref[...]
    a = c / (1.0 + jnp.exp(-c))
    o_ref[...] = jnp.dot(a, w_ref[...], preferred_element_type=_F32) + b_ref[...]


def _modulation(cc, w_ada, b_ada):
    tn = 1536
    return pl.pallas_call(
        _mod_kernel,
        grid=(DEPTH, 6 * D_MODEL // tn),
        in_specs=[
            pl.BlockSpec((MOD_ROWS, D_MODEL), lambda l, j: (0, 0)),
            pl.BlockSpec((None, D_MODEL, tn), lambda l, j: (l, 0, j)),
            pl.BlockSpec((None, 1, tn), lambda l, j: (l, 0, j)),
        ],
        out_specs=pl.BlockSpec((None, MOD_ROWS, tn), lambda l, j: (l, 0, j)),
        out_shape=jax.ShapeDtypeStruct((DEPTH, MOD_ROWS, 6 * D_MODEL), _F32),
        compiler_params=_params(("arbitrary", "arbitrary")),
        name="adaln_mod",
    )(cc, w_ada, b_ada.reshape(DEPTH, 1, 6 * D_MODEL))


def _norm_modulate(x, g, shift, scale):
    ms = jnp.mean(x * x, axis=-1, keepdims=True)
    return (x * lax.rsqrt(ms + NORM_EPS)) * (g * (1.0 + scale)) + shift


def _rope(z, cos, sin_up, sin_dn):
    return z * cos + pltpu.roll(z, 16, 1) * sin_up + pltpu.roll(z, LANES - 16, 1) * sin_dn


def _head_rms(z, gain, lo):
    z2 = z * z
    s_all = jnp.sum(z2, axis=-1, keepdims=True)
    s_lo = jnp.sum(jnp.where(lo, z2, 0.0), axis=-1, keepdims=True)
    ms = jnp.where(lo, s_lo, s_all - s_lo) * (1.0 / HEAD_DIM)
    return (z * lax.rsqrt(ms + NORM_EPS)) * gain


def _pad_heads(z, lo, fill):
    sw = pltpu.roll(z, HEAD_DIM, 1)
    return jnp.concatenate([jnp.where(lo, z, fill), jnp.where(lo, sw, fill)], axis=1).astype(_BF16)


def _sigmoid(z):
    return 1.0 / (1.0 + jnp.exp(-z))


def _in_proj_kernel(xl_ref, xc_ref, mod_ref, g_ref, w_ref, rope_ref, qg_ref, kg_ref,
                    qa_ref, ka_ref, va_ref, qb_ref, kb_ref, vb_ref, ga_ref, gb_ref):
    shift = mod_ref[:, 0:D_MODEL]
    scale = mod_ref[:, D_MODEL:2 * D_MODEL]
    lo = lax.broadcasted_iota(jnp.int32, (1, LANES), 1) < HEAD_DIM
    qscale = HEAD_DIM ** -0.5 * LOG2E
    qg = qg_ref[...]
    for rows in _sub_tiles(SUB_IN_PROJ):
        x = _pick_stream(xl_ref, xc_ref, rows)
        h = _norm_modulate(x, g_ref[...], shift, scale).astype(_BF16)
        cos = rope_ref[rows, 0:LANES]
        sin_up = rope_ref[rows, LANES:2 * LANES]
        sin_dn = rope_ref[rows, 2 * LANES:3 * LANES]

        def proj(c0, width):
            return jnp.dot(h, w_ref[:, c0:c0 + width], preferred_element_type=_F32)

        c = 0
        z = proj(c, Q_W)
        for j in range(Q_W // LANES):
            zj = _rope(z[:, j * LANES:(j + 1) * LANES], cos, sin_up, sin_dn)
            qa_ref[rows, j * LANES:(j + 1) * LANES] = (zj * qscale).astype(_BF16)
        c += Q_W
        z = proj(c, 2 * KV_W)
        ka_ref[rows, :] = _pad_heads(_rope(z[:, 0:KV_W], cos, sin_up, sin_dn), lo, 0.0)
        va_ref[rows, :] = _pad_heads(z[:, KV_W:2 * KV_W], lo, 1.0)
        c += 2 * KV_W
        z = proj(c, Q_W)
        for j in range(Q_W // LANES):
            zj = _rope(_head_rms(z[:, j * LANES:(j + 1) * LANES], qg, lo), cos, sin_up, sin_dn)
            qb_ref[rows, j * LANES:(j + 1) * LANES] = (zj * qscale).astype(_BF16)
        c += Q_W
        z = proj(c, 2 * KV_W)
        zk = _head_rms(z[:, 0:KV_W], kg_ref[...], lo)
        kb_ref[rows, :] = _pad_heads(_rope(zk, cos, sin_up, sin_dn), lo, 0.0)
        vb_ref[rows, :] = _pad_heads(z[:, KV_W:2 * KV_W], lo, 1.0)
        c += 2 * KV_W
        ga_ref[rows, :] = _sigmoid(proj(c, D_MODEL)).astype(_BF16)
        c += D_MODEL
        gb_ref[rows, :] = _sigmoid(proj(c, D_MODEL)).astype(_BF16)


def _in_proj(layer, x_lat, x_ctx, ctx_tile0, mod_l, g1, w_in, rope_tab, qg, kg):
    tile = lambda w: pl.BlockSpec((TM, w), lambda i: (i, 0))
    rope_spec = pl.BlockSpec(
        (TM, 3 * LANES), lambda i: (jnp.where(i < LAT_TILES, i % TILES_PER_SEQ, TILES_PER_SEQ), 0))
    widths = [Q_W, PAD_W, PAD_W, Q_W, PAD_W, PAD_W, D_MODEL, D_MODEL]
    return pl.pallas_call(
        _in_proj_kernel,
        grid=(ALL_TILES,),
        in_specs=[*_split_specs(D_MODEL, ctx_tile0), _mod_spec(), _resident((1, D_MODEL), layer),
                  _resident((D_MODEL, IN_COLS), layer), rope_spec,
                  _resident((1, LANES), layer), _resident((1, LANES), layer)],
        out_specs=[tile(w) for w in widths],
        out_shape=[jax.ShapeDtypeStruct((N_TOK, w), _BF16) for w in widths],
        compiler_params=_params(("arbitrary",)),
        name="in_proj",
    )(x_lat, x_ctx, mod_l, g1, w_in, rope_tab, qg, kg)


def _stack_heads(q, g):
    pairs = [q[:, (2 * g + t) * LANES:(2 * g + t + 1) * LANES] for t in range(2)]
    swapped = [pltpu.roll(p.astype(_F32), HEAD_DIM, 1).astype(_BF16) for p in pairs]
    return jnp.concatenate(pairs + swapped, axis=0)


def _store_heads(o_ref, g, o, extra_den=None):
    tq = o.shape[0] // HEADS_PER_KV
    lo = lax.broadcasted_iota(jnp.int32, (1, LANES), 1) < HEAD_DIM
    for t in range(2):
        a = slice(t * tq, (t + 1) * tq)
        b = slice((2 + t) * tq, (3 + t) * tq)
        same = jnp.where(lo, o[a], o[b])
        cross = pltpu.roll(jnp.where(lo, o[b], o[a]), HEAD_DIM, 1)
        if extra_den is None:
            pair = jnp.where(lo, same / cross, cross / same)
        else:
            ea, eb = extra_den[a], extra_den[b]
            pair = jnp.where(lo, same / (cross + ea), cross / (same + eb))
        o_ref[:, (2 * g + t) * LANES:(2 * g + t + 1) * LANES] = pair.astype(_BF16)


def _sink_column(sink_ref, g, tq):
    blk = lax.broadcasted_iota(jnp.int32, (HEADS_PER_KV * tq, 1), 0) // tq
    col = jnp.full((HEADS_PER_KV * tq, 1), sink_ref[HEADS_PER_KV * g + HEAD_ORDER[0]], _F32)
    for i in range(1, HEADS_PER_KV):
        col = jnp.where(blk == i, sink_ref[HEADS_PER_KV * g + HEAD_ORDER[i]], col)
    return col * LOG2E


def _fold_lanes(x, op):
    out = x[:, 0:LANES]
    for c in range(1, x.shape[1] // LANES):
        out = op(out, x[:, c * LANES:(c + 1) * LANES])
    return out


def _softmax_pv(scores, values, sink_col, bounded=False):
    if bounded:
        parts = [jnp.exp2(s).astype(_BF16) for s in scores]
        sink_term = None if sink_col is None else jnp.exp2(sink_col)
    else:
        s = scores[0] if len(scores) == 1 else jnp.concatenate(scores, axis=1)
        m = jnp.max(_fold_lanes(s, jnp.maximum), axis=-1, keepdims=True)
        if sink_col is not None:
            m = jnp.maximum(m, sink_col)
        pb = jnp.exp2(s - m).astype(_BF16)
        parts, c0 = [], 0
        for v in values:
            parts.append(pb[:, c0:c0 + v.shape[0]])
            c0 += v.shape[0]
        sink_term = None if sink_col is None else jnp.exp2(sink_col - m)
    out = None
    for p, v in zip(parts, values):
        o = jnp.dot(p, v, preferred_element_type=_F32)
        out = o if out is None else out + o
    return out, sink_term


def _window_bounded_kernel(sink_ref, q_ref, k_ref, v_ref, kc_ref, vc_ref, o_ref):
    j = pl.program_id(1)
    nblk = TQ_WINDOW // BLOCK
    r = lax.broadcasted_iota(jnp.int32, (BLOCK, WIN_KEYS), 0)
    col = lax.broadcasted_iota(jnp.int32, (BLOCK, WIN_KEYS), 1)
    q = q_ref[...]
    for g in range(N_KV):
        cs = slice(g * LANES, (g + 1) * LANES)
        qs = _stack_heads(q, g)
        p_c = jnp.exp2(lax.dot_general(qs, kc_ref[:, cs], _NT, preferred_element_type=_F32))
        o_c = jnp.dot(p_c.astype(_BF16), vc_ref[:, cs], preferred_element_type=_F32)
        sink_term = jnp.exp2(_sink_column(sink_ref, g, BLOCK))
        for blk in range(nblk):
            q0 = j * TQ_WINDOW + blk * BLOCK
            start = pl.multiple_of(jnp.clip(q0 - WINDOW, 0, SEQ - WIN_KEYS), BLOCK)
            bias = jnp.where(jnp.abs((start + col) - (q0 + r)) <= WINDOW, 0.0, NEG_INF)
            rows = [slice(i * TQ_WINDOW + blk * BLOCK, i * TQ_WINDOW + (blk + 1) * BLOCK)
                    for i in range(HEADS_PER_KV)]
            s_w = lax.dot_general(jnp.concatenate([qs[rw] for rw in rows], axis=0),
                                  k_ref[pl.ds(start, WIN_KEYS), cs], _NT,
                                  preferred_element_type=_F32)
            p_w = jnp.concatenate(
                [jnp.exp2(s_w[i * BLOCK:(i + 1) * BLOCK] + bias) for i in range(HEADS_PER_KV)], axis=0)
            o = (jnp.dot(p_w.astype(_BF16), v_ref[pl.ds(start, WIN_KEYS), cs],
                         preferred_element_type=_F32)
                 + jnp.concatenate([o_c[rw] for rw in rows], axis=0))
            _store_heads(o_ref.at[blk * BLOCK:(blk + 1) * BLOCK, :], g, o, sink_term)


def _window_kernel(sink_ref, q_ref, k_ref, v_ref, kc_ref, vc_ref, o_ref, *, bounded):
    j = pl.program_id(1)
    r = lax.broadcasted_iota(jnp.int32, (BLOCK, WIN_KEYS), 0)
    col = lax.broadcasted_iota(jnp.int32, (BLOCK, WIN_KEYS), 1)
    for blk in range(TQ_WINDOW // BLOCK):
        q0 = j * TQ_WINDOW + blk * BLOCK
        start = pl.multiple_of(jnp.clip(q0 - WINDOW, 0, SEQ - WIN_KEYS), BLOCK)
        bias = jnp.where(jnp.abs((start + col) - (q0 + r)) <= WINDOW, 0.0, NEG_INF)
        rows = slice(blk * BLOCK, (blk + 1) * BLOCK)
        q = q_ref[rows, :]
        for g in range(N_KV):
            qs = _stack_heads(q, g)
            cs = slice(g * LANES, (g + 1) * LANES)
            kw = k_ref[pl.ds(start, WIN_KEYS), cs]
            vw = v_ref[pl.ds(start, WIN_KEYS), cs]
            s_w = lax.dot_general(qs, kw, _NT, preferred_element_type=_F32)
            s_w = jnp.concatenate(
                [s_w[i * BLOCK:(i + 1) * BLOCK] + bias for i in range(HEADS_PER_KV)], axis=0)
            s_c = lax.dot_general(qs, kc_ref[:, cs], _NT, preferred_element_type=_F32)
            o, sink_term = _softmax_pv([s_w, s_c], [vw, vc_ref[:, cs]],
                                       _sink_column(sink_ref, g, BLOCK), bounded)
            _store_heads(o_ref.at[rows, :], g, o, sink_term)


def _window_attention(sink, qa, ka, va, bounded):
    nq = SEQ // TQ_WINDOW
    lat_kv = pl.BlockSpec((SEQ, PAD_W), lambda b, j: (b, 0))
    ctx_kv = pl.BlockSpec((CTX_LEN, PAD_W), lambda b, j: (N_LAT // CTX_LEN + b, 0))
    return pl.pallas_call(
        _window_bounded_kernel if bounded else functools.partial(_window_kernel, bounded=False),
        grid=(BATCH, nq),
        in_specs=[pl.BlockSpec(memory_space=pltpu.SMEM),
                  pl.BlockSpec((TQ_WINDOW, Q_W), lambda b, j: (b * nq + j, 0)),
                  lat_kv, lat_kv, ctx_kv, ctx_kv],
        out_specs=pl.BlockSpec((TQ_WINDOW, Q_W), lambda b, j: (b * nq + j, 0)),
        out_shape=jax.ShapeDtypeStruct((N_LAT, Q_W), _BF16),
        compiler_params=_params(("arbitrary", "arbitrary")),
        name="window_attn_bounded" if bounded else "window_attn",
    )(sink, qa, ka, va, ka, va)


def _dense_kernel(q_ref, k_ref, v_ref, kc_ref, vc_ref, o_ref, acc_ref, m_ref):
    qs = _stack_heads(q_ref[...], 0)
    m_ref[...] = jnp.full(m_ref.shape, -jnp.inf, _F32)
    acc_ref[...] = jnp.zeros(acc_ref.shape, _F32)

    def update(k, v):
        s = lax.dot_general(qs, k, _NT, preferred_element_type=_F32)
        m_prev = m_ref[...]
        m_new = jnp.maximum(m_prev, jnp.max(s, axis=-1, keepdims=True))
        p = jnp.exp2(s - m_new[:, 0:1])
        acc_ref[...] = (jnp.exp2(m_prev - m_new) * acc_ref[...]
                        + jnp.dot(p.astype(_BF16), v, preferred_element_type=_F32))
        m_ref[...] = m_new

    def chunk(j, carry):
        start = pl.multiple_of(j * TK_UNBOUNDED, TK_UNBOUNDED)
        update(k_ref[pl.ds(start, TK_UNBOUNDED), :], v_ref[pl.ds(start, TK_UNBOUNDED), :])
        return carry

    lax.fori_loop(0, SEQ // TK_UNBOUNDED, chunk, 0)
    update(kc_ref[...], vc_ref[...])
    _store_heads(o_ref, 0, acc_ref[...])


def _dense_bounded_kernel(q_ref, k_ref, v_ref, kc_ref, vc_ref, o_ref, acc_ref):
    qs = _stack_heads(q_ref[...], 0)

    def term(k, v):
        p = jnp.exp2(lax.dot_general(qs, k, _NT, preferred_element_type=_F32))
        return jnp.dot(p.astype(_BF16), v, preferred_element_type=_F32)

    acc_ref[...] = term(kc_ref[...], vc_ref[...])

    def chunk(j, carry):
        for u in range(TK_DENSE // TK_SUB):
            start = pl.multiple_of(j * TK_DENSE + u * TK_SUB, TK_SUB)
            acc_ref[...] += term(k_ref[pl.ds(start, TK_SUB), :], v_ref[pl.ds(start, TK_SUB), :])
        return carry

    lax.fori_loop(0, SEQ // TK_DENSE, chunk, 0)
    _store_heads(o_ref, 0, acc_ref[...])


def _dense_attention(qb, kb, vb, bounded):
    tq = TQ_DENSE if bounded else TQ_DENSE_UNBOUNDED
    nq = SEQ // tq
    lat_kv = pl.BlockSpec((SEQ, LANES), lambda b, g, i: (b, g))
    ctx_kv = pl.BlockSpec((CTX_LEN, LANES), lambda b, g, i: (N_LAT // CTX_LEN + b, g))
    stat = pltpu.VMEM((HEADS_PER_KV * tq, LANES), _F32)
    return pl.pallas_call(
        _dense_bounded_kernel if bounded else _dense_kernel,
        grid=(BATCH, N_KV, nq),
        in_specs=[pl.BlockSpec((tq, 2 * LANES), lambda b, g, i: (b * nq + i, g)),
                  lat_kv, lat_kv, ctx_kv, ctx_kv],
        out_specs=pl.BlockSpec((tq, 2 * LANES), lambda b, g, i: (b * nq + i, g)),
        out_shape=jax.ShapeDtypeStruct((N_LAT, Q_W), _BF16),
        scratch_shapes=[stat] if bounded else [stat, stat],
        compiler_params=_params(("arbitrary", "arbitrary", "arbitrary")),
        name="dense_attn_bounded" if bounded else "dense_attn",
    )(qb, kb, vb, kb, vb)


def _ctx_kernel(sink_ref, qa_ref, ka_ref, va_ref, qb_ref, kb_ref, vb_ref, oa_ref, ob_ref):
    for q_ref, k_ref, v_ref, o_ref, use_sink in ((qa_ref, ka_ref, va_ref, oa_ref, True),
                                                 (qb_ref, kb_ref, vb_ref, ob_ref, False)):
        for blk in range(CTX_LEN // BLOCK):
            rows = slice(blk * BLOCK, (blk + 1) * BLOCK)
            q = q_ref[rows, :]
            for g in range(N_KV):
                cs = slice(g * LANES, (g + 1) * LANES)
                s = lax.dot_general(_stack_heads(q, g), k_ref[:, cs], _NT,
                                    preferred_element_type=_F32)
                sink_col = _sink_column(sink_ref, g, BLOCK) if use_sink else None
                _store_heads(o_ref.at[rows, :], g, *_softmax_pv([s], [v_ref[:, cs]], sink_col))


def _ctx_attention(sink, qa, ka, va, qb, kb, vb):
    row = lambda w: pl.BlockSpec((CTX_LEN, w), lambda b: (N_LAT // CTX_LEN + b, 0))
    out = pl.BlockSpec((CTX_LEN, Q_W), lambda b: (b, 0))
    return pl.pallas_call(
        _ctx_kernel,
        grid=(BATCH,),
        in_specs=[pl.BlockSpec(memory_space=pltpu.SMEM),
                  row(Q_W), row(PAD_W), row(PAD_W), row(Q_W), row(PAD_W), row(PAD_W)],
        out_specs=[out, out],
        out_shape=[jax.ShapeDtypeStruct((N_CTX, Q_W), _BF16)] * 2,
        compiler_params=_params(("arbitrary",)),
        name="ctx_attn",
    )(sink, qa, ka, va, qb, kb, vb)


def _merge_kernel(xl_ref, xc_ref, mod_ref, yal_ref, yac_ref, ybl_ref, ybc_ref, ga_ref, gb_ref,
                  wa_ref, wb_ref, wo_ref, o_ref):
    gate = mod_ref[:, 2 * D_MODEL:3 * D_MODEL]
    for rows in _sub_tiles(SUB_MERGE):
        ya = _pick_stream(yal_ref, yac_ref, rows)
        yb = _pick_stream(ybl_ref, ybc_ref, rows)
        pa = jnp.dot(ya, wa_ref[...], preferred_element_type=_F32)
        pb = jnp.dot(yb, wb_ref[...], preferred_element_type=_F32)
        m = ga_ref[rows, :].astype(_F32) * pa + gb_ref[rows, :].astype(_F32) * pb
        y = jnp.dot(m.astype(_BF16), wo_ref[...], preferred_element_type=_F32)
        o_ref[rows, :] = _pick_stream(xl_ref, xc_ref, rows) + gate * y


def _merge(layer, x_lat, x_ctx, ctx_tile0, mod_l, ya, yac, yb, ybc, ga, gb, w_pa, w_pb, w_o, n_tiles):
    tile = lambda w: pl.BlockSpec((TM, w), lambda i: (i, 0))
    return pl.pallas_call(
        _merge_kernel,
        grid=(n_tiles,),
        in_specs=[*_split_specs(D_MODEL, ctx_tile0), _mod_spec(),
                  *_split_specs(Q_W, 0), *_split_specs(Q_W, 0), tile(D_MODEL), tile(D_MODEL),
                  _resident((Q_W, D_MODEL), layer), _resident((Q_W, D_MODEL), layer),
                  _resident((D_MODEL, D_MODEL), layer)],
        out_specs=tile(D_MODEL),
        out_shape=jax.ShapeDtypeStruct((n_tiles * TM, D_MODEL), _F32),
        compiler_params=_params(("arbitrary",)),
        name="merge",
    )(x_lat, x_ctx, mod_l, ya, yac, yb, ybc, ga, gb, w_pa, w_pb, w_o)


def _ffn_kernel(x_ref, mod_ref, g_ref, wg_ref, wu_ref, wd_ref, gf_ref, o_ref, *, final_norm):
    shift = mod_ref[:, 3 * D_MODEL:4 * D_MODEL]
    scale = mod_ref[:, 4 * D_MODEL:5 * D_MODEL]
    gate = mod_ref[:, 5 * D_MODEL:6 * D_MODEL]
    for rows in _sub_tiles(SUB_FFN):
        x = x_ref[rows, :]
        h = _norm_modulate(x, g_ref[...], shift, scale).astype(_BF16)
        a = jnp.dot(h, wg_ref[...], preferred_element_type=_F32)
        u = jnp.dot(h, wu_ref[...], preferred_element_type=_F32)
        act = ((a / (1.0 + jnp.exp(-a))) * u).astype(_BF16)
        y = x + gate * jnp.dot(act, wd_ref[...], preferred_element_type=_F32)
        if final_norm:
            ms = jnp.mean(y * y, axis=-1, keepdims=True)
            y = (y * lax.rsqrt(ms + NORM_EPS)) * gf_ref[...]
        o_ref[rows, :] = y


def _ffn(layer, x_all, mod_l, g2, w_gate, w_up, w_down, gf, n_tiles, final_norm):
    tile = pl.BlockSpec((TM, D_MODEL), lambda i: (i, 0))
    return pl.pallas_call(
        functools.partial(_ffn_kernel, final_norm=final_norm),
        grid=(n_tiles,),
        in_specs=[tile, _mod_spec(), _resident((1, D_MODEL), layer),
                  _resident((D_MODEL, D_FF), layer), _resident((D_MODEL, D_FF), layer),
                  _resident((D_FF, D_MODEL), layer), _resident((1, D_MODEL))],
        out_specs=tile,
        out_shape=jax.ShapeDtypeStruct((n_tiles * TM, D_MODEL), _F32),
        compiler_params=_params(("arbitrary",)),
        name="ffn",
    )(x_all, mod_l, g2, w_gate, w_up, w_down, gf)


def _rope_table():
    rows = SEQ // GRID_W
    row = jnp.repeat(jnp.arange(rows, dtype=_F32), GRID_W)
    col = jnp.tile(jnp.arange(GRID_W, dtype=_F32), rows)
    axis_dims = HEAD_DIM // 2
    inv = ROPE_THETA ** (-jnp.arange(0, axis_dims, 2, dtype=_F32) / axis_dims)
    ang = jnp.stack([row[:, None] * inv, col[:, None] * inv], axis=1)
    cos, sin = jnp.cos(ang), jnp.sin(ang)
    lane = np.arange(LANES)
    axis = (lane % HEAD_DIM) // axis_dims
    upper = ((lane % axis_dims) // (axis_dims // 2)) == 1
    j = lane % (axis_dims // 2)
    cos_l = cos[:, axis, j]
    sin_l = sin[:, axis, j]
    sin_up = jnp.where(upper[None, :], sin_l, 0.0)
    sin_dn = jnp.where(upper[None, :], 0.0, -sin_l)
    lat = jnp.concatenate([cos_l, sin_up, sin_dn], axis=1)
    ident = jnp.concatenate([jnp.ones((TM, LANES), _F32), jnp.zeros((TM, 2 * LANES), _F32)], axis=1)
    return jnp.concatenate([lat, ident], axis=0)


def _head_norm_kernel(q_ref, k_ref, ones_ref, qmax_ref, kmax_ref):
    @pl.when(pl.program_id(0) == 0)
    def _():
        qmax_ref[...] = jnp.zeros(qmax_ref.shape, _F32)
        kmax_ref[...] = jnp.zeros(kmax_ref.shape, _F32)

    w = ones_ref.shape[0]
    for x_ref, o_ref in ((q_ref, qmax_ref), (k_ref, kmax_ref)):
        for c in range(0, x_ref.shape[1], w):
            x = x_ref[:, c:c + w].astype(_F32)
            ss = jnp.dot((x * x).astype(_BF16), ones_ref[...], preferred_element_type=_F32)
            o_ref[:, c:c + w] = jnp.maximum(o_ref[:, c:c + w],
                                            jnp.max(ss.reshape(TM // 8, 8, w), axis=0))


def _window_logit_bound(qa, ka, sink):
    head = np.arange(PAD_W) // HEAD_DIM
    head_ones = jnp.asarray(head[:, None] == head[None, :], _BF16)
    tile = lambda w: pl.BlockSpec((TM, w), lambda i: (i, 0))
    top = lambda w: pl.BlockSpec((8, w), lambda i: (0, 0))
    qmax, kmax = pl.pallas_call(
        _head_norm_kernel,
        grid=(ALL_TILES,),
        in_specs=[tile(Q_W), tile(PAD_W), _resident((PAD_W, PAD_W))],
        out_specs=[top(Q_W), top(PAD_W)],
        out_shape=[jax.ShapeDtypeStruct((8, Q_W), _F32), jax.ShapeDtypeStruct((8, PAD_W), _F32)],
        compiler_params=_params(("arbitrary",)),
        name="head_norms",
    )(qa, ka, head_ones)
    qk = jnp.sqrt(jnp.max(qmax) * jnp.max(kmax)) * 1.01
    return jnp.maximum(qk, jnp.max(jnp.abs(sink)) * LOG2E)


def kernel(x, c, ctx, c_ctx, w_ada, b_ada, norm1_g, norm2_g, w_in, q_norm_g, k_norm_g, sink_a,
           w_proj_a, w_proj_b, w_out, w_ffn_gate, w_ffn_up, w_ffn_down, final_norm_g):
    assert x.shape == (BATCH, SEQ, D_MODEL) and ctx.shape == (BATCH, CTX_LEN, D_MODEL)
    x_lat, x_ctx, ctx_tile0 = x.reshape(N_LAT, D_MODEL), ctx.reshape(N_CTX, D_MODEL), 0
    cc = jnp.concatenate([c, c_ctx[None, :], jnp.zeros((MOD_ROWS - BATCH - 1, D_MODEL), _F32)], axis=0)
    mod = _modulation(cc, w_ada, b_ada).reshape(DEPTH, MOD_ROWS, 1, 6 * D_MODEL)
    rope_tab = _rope_table()
    gf = final_norm_g.reshape(1, D_MODEL)
    w_in, w_proj_a, w_proj_b, w_out, w_ffn_gate, w_ffn_up, w_ffn_down = (
        w.astype(_BF16) for w in (w_in, w_proj_a, w_proj_b, w_out, w_ffn_gate, w_ffn_up, w_ffn_down))
    norm1_g = norm1_g.reshape(DEPTH, 1, D_MODEL)
    norm2_g = norm2_g.reshape(DEPTH, 1, D_MODEL)
    qg = jnp.tile(q_norm_g, (1, 2)).reshape(DEPTH, 1, LANES)
    kg = jnp.tile(k_norm_g, (1, 2)).reshape(DEPTH, 1, LANES)
    out = None
    for l in range(DEPTH):
        last = l == DEPTH - 1
        qa, ka, va, qb, kb, vb, ga, gb = _in_proj(
            l, x_lat, x_ctx, ctx_tile0, mod[l], norm1_g, w_in, rope_tab, qg, kg)
        ya = lax.cond(_window_logit_bound(qa, ka, sink_a[l]) <= LOGIT_BOUND,
                      functools.partial(_window_attention, bounded=True),
                      functools.partial(_window_attention, bounded=False), sink_a[l], qa, ka, va)
        score_bound = HEAD_DIM ** 0.5 * jnp.max(jnp.abs(q_norm_g[l])) * jnp.max(jnp.abs(k_norm_g[l]))
        yb = lax.cond(score_bound <= SCORE_BOUND,
                      functools.partial(_dense_attention, bounded=True),
                      functools.partial(_dense_attention, bounded=False), qb, kb, vb)
        n_tiles = LAT_TILES if last else ALL_TILES
        yac, ybc = (ya, yb) if last else _ctx_attention(sink_a[l], qa, ka, va, qb, kb, vb)
        x_mid = _merge(l, x_lat, x_ctx, ctx_tile0, mod[l], ya, yac, yb, ybc, ga, gb,
                       w_proj_a, w_proj_b, w_out, n_tiles)
        x_new = _ffn(l, x_mid, mod[l], norm2_g, w_ffn_gate, w_ffn_up, w_ffn_down, gf, n_tiles, last)
        if last:
            out = x_new
        else:
            x_lat, x_ctx, ctx_tile0 = x_new, x_new, LAT_TILES
    return out.reshape(BATCH, SEQ, D_MODEL)
```

```python
import functools

import jax
import jax.numpy as jnp
import numpy as np
from jax import lax
from jax.experimental import pallas as pl
from jax.experimental.pallas import tpu as pltpu

D_MODEL = 1024
BATCH = 4
SEQ = 4096
DEPTH = 4
CTX_LEN = 256
GRID_W = 64
HEAD_DIM = 64
N_HEADS = 8
N_KV = 2
WINDOW = 128
BLOCK = 128
D_FF = 2816
ROPE_THETA = 10000.0
NORM_EPS = 1e-6
NEG_INF = -1e30
Q_W = N_HEADS * HEAD_DIM
KV_W = N_KV * HEAD_DIM
IN_COLS = 2 * (Q_W + 2 * KV_W) + 2 * D_MODEL
LANES = 128
PAD_W = N_KV * LANES
HEADS_PER_KV = N_HEADS // N_KV
HEAD_ORDER = (0, 2, 1, 3)

N_LAT = BATCH * SEQ
N_CTX = BATCH * CTX_LEN
N_TOK = N_LAT + N_CTX
MOD_ROWS = 8
CTX_MOD_ROW = BATCH

TM = 1024
SUB_IN_PROJ, SUB_MERGE, SUB_FFN = 256, 512, 256
LAT_TILES = N_LAT // TM
ALL_TILES = N_TOK // TM
TILES_PER_SEQ = SEQ // TM
TQ_DENSE = 1024
TQ_DENSE_UNBOUNDED = 128
TQ_WINDOW = 1024
WIN_KEYS = BLOCK + 2 * WINDOW
WIN_Q_BOUNDED = 128
TK_DENSE = 2048
TK_UNBOUNDED = 1024
TK_SUB = 256
LOG2E = 1.4426950408889634
SCORE_BOUND = 60.0
LOGIT_BOUND = 80.0
VMEM_LIMIT = 56 * 1024 * 1024

_F32 = jnp.float32
_BF16 = jnp.bfloat16
_NT = (((1,), (1,)), ((), ()))


def _params(sem):
    return pltpu.CompilerParams(dimension_semantics=sem, vmem_limit_bytes=VMEM_LIMIT)


def _resident(shape, layer=None):
    nd = len(shape)
    if layer is None:
        return pl.BlockSpec(shape, lambda *_: (0,) * nd, pipeline_mode=pl.Buffered(1))
    return pl.BlockSpec((None,) + tuple(shape), lambda *_: (layer,) + (0,) * nd,
                        pipeline_mode=pl.Buffered(1))


def _mod_row_of_tile(i):
    return jnp.minimum(i // TILES_PER_SEQ, CTX_MOD_ROW)


def _mod_spec():
    return pl.BlockSpec((None, 1, 6 * D_MODEL), lambda i: (_mod_row_of_tile(i), 0, 0))


def _split_specs(width, ctx_tile0):
    lat = pl.BlockSpec((TM, width), lambda i: (jnp.minimum(i, LAT_TILES - 1), 0))
    ctx = pl.BlockSpec((TM, width), lambda i: (jnp.maximum(i - LAT_TILES, 0) + ctx_tile0, 0))
    return lat, ctx


def _pick_stream(lat_ref, ctx_ref, rows):
    return jnp.where(pl.program_id(0) < LAT_TILES, lat_ref[rows, :], ctx_ref[rows, :])


def _sub_tiles(sub_m):
    return [slice(s * sub_m, (s + 1) * sub_m) for s in range(TM // sub_m)]


def _mod_kernel(c_ref, w_ref, b_ref, o_ref):
    c = c_ref[...]
    a = c / (1.0 + jnp.exp(-c))
    o_ref[...] = jnp.dot(a, w_ref[...], preferred_element_type=_F32) + b_ref[...]


def _modulation(cc, w_ada, b_ada):
    tn = 3072
    return pl.pallas_call(
        _mod_kernel,
        grid=(DEPTH, 6 * D_MODEL // tn),
        in_specs=[
            pl.BlockSpec((MOD_ROWS, D_MODEL), lambda l, j: (0, 0)),
            pl.BlockSpec((None, D_MODEL, tn), lambda l, j: (l, 0, j)),
            pl.BlockSpec((None, 1, tn), lambda l, j: (l, 0, j)),
        ],
        out_specs=pl.BlockSpec((None, MOD_ROWS, tn), lambda l, j: (l, 0, j)),
        out_shape=jax.ShapeDtypeStruct((DEPTH, MOD_ROWS, 6 * D_MODEL), _F32),
        compiler_params=_params(("arbitrary", "arbitrary")),
        name="adaln_mod",
    )(cc, w_ada, b_ada.reshape(DEPTH, 1, 6 * D_MODEL))


def _norm_modulate(x, g, shift, scale):
    ms = jnp.mean(x * x, axis=-1, keepdims=True)
    return (x * lax.rsqrt(ms + NORM_EPS)) * (g * (1.0 + scale)) + shift


def _rope(z, cos, sin_up, sin_dn):
    return z * cos + pltpu.roll(z, 16, 1) * sin_up + pltpu.roll(z, LANES - 16, 1) * sin_dn


def _head_rms(z, gain, lo):
    z2 = z * z
    s_all = jnp.sum(z2, axis=-1, keepdims=True)
    s_lo = jnp.sum(jnp.where(lo, z2, 0.0), axis=-1, keepdims=True)
    ms = jnp.where(lo, s_lo, s_all - s_lo) * (1.0 / HEAD_DIM)
    return (z * lax.rsqrt(ms + NORM_EPS)) * gain


def _pad_heads(z, lo, fill):
    sw = pltpu.roll(z, HEAD_DIM, 1)
    return jnp.concatenate([jnp.where(lo, z, fill), jnp.where(lo, sw, fill)], axis=1).astype(_BF16)


def _sigmoid(z):
    return 1.0 / (1.0 + jnp.exp(-z))


def _in_proj_kernel(xl_ref, xc_ref, mod_ref, g_ref, w_ref, rope_ref, qg_ref, kg_ref,
                    qa_ref, ka_ref, va_ref, qb_ref, kb_ref, vb_ref, ga_ref, gb_ref):
    shift = mod_ref[:, 0:D_MODEL]
    scale = mod_ref[:, D_MODEL:2 * D_MODEL]
    lo = lax.broadcasted_iota(jnp.int32, (1, LANES), 1) < HEAD_DIM
    qscale = HEAD_DIM ** -0.5 * LOG2E
    qg = qg_ref[...]
    for rows in _sub_tiles(SUB_IN_PROJ):
        x = _pick_stream(xl_ref, xc_ref, rows)
        h = _norm_modulate(x, g_ref[...], shift, scale).astype(_BF16)
        cos = rope_ref[rows, 0:LANES]
        sin_up = rope_ref[rows, LANES:2 * LANES]
        sin_dn = rope_ref[rows, 2 * LANES:3 * LANES]

        def proj(c0, width):
            return jnp.dot(h, w_ref[:, c0:c0 + width], preferred_element_type=_F32)

        c = 0
        z = proj(c, Q_W)
        for j in range(Q_W // LANES):
            zj = _rope(z[:, j * LANES:(j + 1) * LANES], cos, sin_up, sin_dn)
            qa_ref[rows, j * LANES:(j + 1) * LANES] = (zj * qscale).astype(_BF16)
        c += Q_W
        z = proj(c, 2 * KV_W)
        ka_ref[rows, :] = _pad_heads(_rope(z[:, 0:KV_W], cos, sin_up, sin_dn), lo, 0.0)
        va_ref[rows, :] = _pad_heads(z[:, KV_W:2 * KV_W], lo, 1.0)
        c += 2 * KV_W
        z = proj(c, Q_W)
        for j in range(Q_W // LANES):
            zj = _rope(_head_rms(z[:, j * LANES:(j + 1) * LANES], qg, lo), cos, sin_up, sin_dn)
            qb_ref[rows, j * LANES:(j + 1) * LANES] = (zj * qscale).astype(_BF16)
        c += Q_W
        z = proj(c, 2 * KV_W)
        zk = _head_rms(z[:, 0:KV_W], kg_ref[...], lo)
        kb_ref[rows, :] = _pad_heads(_rope(zk, cos, sin_up, sin_dn), lo, 0.0)
        vb_ref[rows, :] = _pad_heads(z[:, KV_W:2 * KV_W], lo, 1.0)
        c += 2 * KV_W
        ga_ref[rows, :] = _sigmoid(proj(c, D_MODEL)).astype(_BF16)
        c += D_MODEL
        gb_ref[rows, :] = _sigmoid(proj(c, D_MODEL)).astype(_BF16)


def _in_proj(layer, x_lat, x_ctx, ctx_tile0, mod_l, g1, w_in, rope_tab, qg, kg):
    tile = lambda w: pl.BlockSpec((TM, w), lambda i: (i, 0))
    rope_spec = pl.BlockSpec(
        (TM, 3 * LANES), lambda i: (jnp.where(i < LAT_TILES, i % TILES_PER_SEQ, TILES_PER_SEQ), 0))
    widths = [Q_W, PAD_W, PAD_W, Q_W, PAD_W, PAD_W, D_MODEL, D_MODEL]
    return pl.pallas_call(
        _in_proj_kernel,
        grid=(ALL_TILES,),
        in_specs=[*_split_specs(D_MODEL, ctx_tile0), _mod_spec(), _resident((1, D_MODEL), layer),
                  _resident((D_MODEL, IN_COLS), layer), rope_spec,
                  _resident((1, LANES), layer), _resident((1, LANES), layer)],
        out_specs=[tile(w) for w in widths],
        out_shape=[jax.ShapeDtypeStruct((N_TOK, w), _BF16) for w in widths],
        compiler_params=_params(("arbitrary",)),
        name="in_proj",
    )(x_lat, x_ctx, mod_l, g1, w_in, rope_tab, qg, kg)


def _stack_heads(q, g):
    pairs = [q[:, (2 * g + t) * LANES:(2 * g + t + 1) * LANES] for t in range(2)]
    swapped = [pltpu.roll(p.astype(_F32), HEAD_DIM, 1).astype(_BF16) for p in pairs]
    return jnp.concatenate(pairs + swapped, axis=0)


def _store_heads(o_ref, g, o, extra_den=None):
    tq = o.shape[0] // HEADS_PER_KV
    lo = lax.broadcasted_iota(jnp.int32, (1, LANES), 1) < HEAD_DIM
    for t in range(2):
        a = slice(t * tq, (t + 1) * tq)
        b = slice((2 + t) * tq, (3 + t) * tq)
        same = jnp.where(lo, o[a], o[b])
        cross = pltpu.roll(jnp.where(lo, o[b], o[a]), HEAD_DIM, 1)
        if extra_den is None:
            pair = jnp.where(lo, same / cross, cross / same)
        else:
            ea, eb = extra_den[a], extra_den[b]
            pair = jnp.where(lo, same / (cross + ea), cross / (same + eb))
        o_ref[:, (2 * g + t) * LANES:(2 * g + t + 1) * LANES] = pair.astype(_BF16)


def _sink_column(sink_ref, g, tq):
    blk = lax.broadcasted_iota(jnp.int32, (HEADS_PER_KV * tq, 1), 0) // tq
    col = jnp.full((HEADS_PER_KV * tq, 1), sink_ref[HEADS_PER_KV * g + HEAD_ORDER[0]], _F32)
    for i in range(1, HEADS_PER_KV):
        col = jnp.where(blk == i, sink_ref[HEADS_PER_KV * g + HEAD_ORDER[i]], col)
    return col * LOG2E


def _fold_lanes(x, op):
    out = x[:, 0:LANES]
    for c in range(1, x.shape[1] // LANES):
        out = op(out, x[:, c * LANES:(c + 1) * LANES])
    return out


def _softmax_pv(scores, values, sink_col, bounded=False):
    if bounded:
        parts = [jnp.exp2(s).astype(_BF16) for s in scores]
        sink_term = None if sink_col is None else jnp.exp2(sink_col)
    else:
        s = scores[0] if len(scores) == 1 else jnp.concatenate(scores, axis=1)
        m = jnp.max(_fold_lanes(s, jnp.maximum), axis=-1, keepdims=True)
        if sink_col is not None:
            m = jnp.maximum(m, sink_col)
        pb = jnp.exp2(s - m).astype(_BF16)
        parts, c0 = [], 0
        for v in values:
            parts.append(pb[:, c0:c0 + v.shape[0]])
            c0 += v.shape[0]
        sink_term = None if sink_col is None else jnp.exp2(sink_col - m)
    out = None
    for p, v in zip(parts, values):
        o = jnp.dot(p, v, preferred_element_type=_F32)
        out = o if out is None else out + o
    return out, sink_term


def _window_bounded_kernel(sink_ref, q_ref, k_ref, v_ref, kc_ref, vc_ref, o_ref):
    j = pl.program_id(1)
    wq, wk = WIN_Q_BOUNDED, WIN_Q_BOUNDED + 2 * WINDOW
    r = lax.broadcasted_iota(jnp.int32, (wq, wk), 0)
    col = lax.broadcasted_iota(jnp.int32, (wq, wk), 1)
    q = q_ref[...]
    for g in range(N_KV):
        cs = slice(g * LANES, (g + 1) * LANES)
        qs = _stack_heads(q, g)
        p_c = jnp.exp2(lax.dot_general(qs, kc_ref[:, cs], _NT, preferred_element_type=_F32))
        o_c = jnp.dot(p_c.astype(_BF16), vc_ref[:, cs], preferred_element_type=_F32)
        sink_term = jnp.exp2(_sink_column(sink_ref, g, wq))
        for blk in range(TQ_WINDOW // wq):
            q0 = j * TQ_WINDOW + blk * wq
            start = pl.multiple_of(jnp.clip(q0 - WINDOW, 0, SEQ - wk), BLOCK)
            bias = jnp.where(jnp.abs((start + col) - (q0 + r)) <= WINDOW, 0.0, NEG_INF)
            rows = [slice(i * TQ_WINDOW + blk * wq, i * TQ_WINDOW + (blk + 1) * wq)
                    for i in range(HEADS_PER_KV)]
            s_w = lax.dot_general(jnp.concatenate([qs[rw] for rw in rows], axis=0),
                                  k_ref[pl.ds(start, wk), cs], _NT, preferred_element_type=_F32)
            p_w = jnp.concatenate(
                [jnp.exp2(s_w[i * wq:(i + 1) * wq] + bias) for i in range(HEADS_PER_KV)], axis=0)
            o = (jnp.dot(p_w.astype(_BF16), v_ref[pl.ds(start, wk), cs], preferred_element_type=_F32)
                 + jnp.concatenate([o_c[rw] for rw in rows], axis=0))
            _store_heads(o_ref.at[blk * wq:(blk + 1) * wq, :], g, o, sink_term)


def _window_kernel(sink_ref, q_ref, k_ref, v_ref, kc_ref, vc_ref, o_ref, *, bounded):
    j = pl.program_id(1)
    r = lax.broadcasted_iota(jnp.int32, (BLOCK, WIN_KEYS), 0)
    col = lax.broadcasted_iota(jnp.int32, (BLOCK, WIN_KEYS), 1)
    for blk in range(TQ_WINDOW // BLOCK):
        q0 = j * TQ_WINDOW + blk * BLOCK
        start = pl.multiple_of(jnp.clip(q0 - WINDOW, 0, SEQ - WIN_KEYS), BLOCK)
        bias = jnp.where(jnp.abs((start + col) - (q0 + r)) <= WINDOW, 0.0, NEG_INF)
        rows = slice(blk * BLOCK, (blk + 1) * BLOCK)
        q = q_ref[rows, :]
        for g in range(N_KV):
            qs = _stack_heads(q, g)
            cs = slice(g * LANES, (g + 1) * LANES)
            kw = k_ref[pl.ds(start, WIN_KEYS), cs]
            vw = v_ref[pl.ds(start, WIN_KEYS), cs]
            s_w = lax.dot_general(qs, kw, _NT, preferred_element_type=_F32)
            s_w = jnp.concatenate(
                [s_w[i * BLOCK:(i + 1) * BLOCK] + bias for i in range(HEADS_PER_KV)], axis=0)
            s_c = lax.dot_general(qs, kc_ref[:, cs], _NT, preferred_element_type=_F32)
            o, sink_term = _softmax_pv([s_w, s_c], [vw, vc_ref[:, cs]],
                                       _sink_column(sink_ref, g, BLOCK), bounded)
            _store_heads(o_ref.at[rows, :], g, o, sink_term)


def _window_attention(sink, qa, ka, va, bounded):
    nq = SEQ // TQ_WINDOW
    lat_kv = pl.BlockSpec((SEQ, PAD_W), lambda b, j: (b, 0))
    ctx_kv = pl.BlockSpec((CTX_LEN, PAD_W), lambda b, j: (N_LAT // CTX_LEN + b, 0))
    return pl.pallas_call(
        _window_bounded_kernel if bounded else functools.partial(_window_kernel, bounded=False),
        grid=(BATCH, nq),
        in_specs=[pl.BlockSpec(memory_space=pltpu.SMEM),
                  pl.BlockSpec((TQ_WINDOW, Q_W), lambda b, j: (b * nq + j, 0)),
                  lat_kv, lat_kv, ctx_kv, ctx_kv],
        out_specs=pl.BlockSpec((TQ_WINDOW, Q_W), lambda b, j: (b * nq + j, 0)),
        out_shape=jax.ShapeDtypeStruct((N_LAT, Q_W), _BF16),
        compiler_params=_params(("arbitrary", "arbitrary")),
        name="window_attn_bounded" if bounded else "window_attn",
    )(sink, qa, ka, va, ka, va)


def _dense_kernel(q_ref, k_ref, v_ref, kc_ref, vc_ref, o_ref, acc_ref, m_ref):
    qs = _stack_heads(q_ref[...], 0)
    m_ref[...] = jnp.full(m_ref.shape, -jnp.inf, _F32)
    acc_ref[...] = jnp.zeros(acc_ref.shape, _F32)

    def update(k, v):
        s = lax.dot_general(qs, k, _NT, preferred_element_type=_F32)
        m_prev = m_ref[...]
        m_new = jnp.maximum(m_prev, jnp.max(s, axis=-1, keepdims=True))
        p = jnp.exp2(s - m_new[:, 0:1])
        acc_ref[...] = (jnp.exp2(m_prev - m_new) * acc_ref[...]
                        + jnp.dot(p.astype(_BF16), v, preferred_element_type=_F32))
        m_ref[...] = m_new

    def chunk(j, carry):
        start = pl.multiple_of(j * TK_UNBOUNDED, TK_UNBOUNDED)
        update(k_ref[pl.ds(start, TK_UNBOUNDED), :], v_ref[pl.ds(start, TK_UNBOUNDED), :])
        return carry

    lax.fori_loop(0, SEQ // TK_UNBOUNDED, chunk, 0)
    update(kc_ref[...], vc_ref[...])
    _store_heads(o_ref, 0, acc_ref[...])


def _dense_bounded_kernel(q_ref, k_ref, v_ref, kc_ref, vc_ref, o_ref, acc_ref):
    qs = _stack_heads(q_ref[...], 0)

    def term(k, v):
        p = jnp.exp2(lax.dot_general(qs, k, _NT, preferred_element_type=_F32))
        return jnp.dot(p.astype(_BF16), v, preferred_element_type=_F32)

    acc_ref[...] = term(kc_ref[...], vc_ref[...])

    def chunk(j, carry):
        for u in range(TK_DENSE // TK_SUB):
            start = pl.multiple_of(j * TK_DENSE + u * TK_SUB, TK_SUB)
            acc_ref[...] += term(k_ref[pl.ds(start, TK_SUB), :], v_ref[pl.ds(start, TK_SUB), :])
        return carry

    lax.fori_loop(0, SEQ // TK_DENSE, chunk, 0)
    _store_heads(o_ref, 0, acc_ref[...])


def _dense_attention(qb, kb, vb, bounded):
    tq = TQ_DENSE if bounded else TQ_DENSE_UNBOUNDED
    nq = SEQ // tq
    lat_kv = pl.BlockSpec((SEQ, LANES), lambda b, g, i: (b, g))
    ctx_kv = pl.BlockSpec((CTX_LEN, LANES), lambda b, g, i: (N_LAT // CTX_LEN + b, g))
    stat = pltpu.VMEM((HEADS_PER_KV * tq, LANES), _F32)
    return pl.pallas_call(
        _dense_bounded_kernel if bounded else _dense_kernel,
        grid=(BATCH, N_KV, nq),
        in_specs=[pl.BlockSpec((tq, 2 * LANES), lambda b, g, i: (b * nq + i, g)),
                  lat_kv, lat_kv, ctx_kv, ctx_kv],
        out_specs=pl.BlockSpec((tq, 2 * LANES), lambda b, g, i: (b * nq + i, g)),
        out_shape=jax.ShapeDtypeStruct((N_LAT, Q_W), _BF16),
        scratch_shapes=[stat] if bounded else [stat, stat],
        compiler_params=_params(("arbitrary", "arbitrary", "arbitrary")),
        name="dense_attn_bounded" if bounded else "dense_attn",
    )(qb, kb, vb, kb, vb)


def _ctx_kernel(sink_ref, qa_ref, ka_ref, va_ref, qb_ref, kb_ref, vb_ref, oa_ref, ob_ref):
    for q_ref, k_ref, v_ref, o_ref, use_sink in ((qa_ref, ka_ref, va_ref, oa_ref, True),
                                                 (qb_ref, kb_ref, vb_ref, ob_ref, False)):
        for blk in range(CTX_LEN // BLOCK):
            rows = slice(blk * BLOCK, (blk + 1) * BLOCK)
            q = q_ref[rows, :]
            for g in range(N_KV):
                cs = slice(g * LANES, (g + 1) * LANES)
                s = lax.dot_general(_stack_heads(q, g), k_ref[:, cs], _NT,
                                    preferred_element_type=_F32)
                sink_col = _sink_column(sink_ref, g, BLOCK) if use_sink else None
                _store_heads(o_ref.at[rows, :], g, *_softmax_pv([s], [v_ref[:, cs]], sink_col))


def _ctx_attention(sink, qa, ka, va, qb, kb, vb):
    row = lambda w: pl.BlockSpec((CTX_LEN, w), lambda b: (N_LAT // CTX_LEN + b, 0))
    out = pl.BlockSpec((CTX_LEN, Q_W), lambda b: (b, 0))
    return pl.pallas_call(
        _ctx_kernel,
        grid=(BATCH,),
        in_specs=[pl.BlockSpec(memory_space=pltpu.SMEM),
                  row(Q_W), row(PAD_W), row(PAD_W), row(Q_W), row(PAD_W), row(PAD_W)],
        out_specs=[out, out],
        out_shape=[jax.ShapeDtypeStruct((N_CTX, Q_W), _BF16)] * 2,
        compiler_params=_params(("arbitrary",)),
        name="ctx_attn",
    )(sink, qa, ka, va, qb, kb, vb)


def _merge_kernel(xl_ref, xc_ref, mod_ref, yal_ref, yac_ref, ybl_ref, ybc_ref, ga_ref, gb_ref,
                  wa_ref, wb_ref, wo_ref, o_ref):
    gate = mod_ref[:, 2 * D_MODEL:3 * D_MODEL]
    for rows in _sub_tiles(SUB_MERGE):
        ya = _pick_stream(yal_ref, yac_ref, rows)
        yb = _pick_stream(ybl_ref, ybc_ref, rows)
        pa = jnp.dot(ya, wa_ref[...], preferred_element_type=_F32)
        pb = jnp.dot(yb, wb_ref[...], preferred_element_type=_F32)
        m = ga_ref[rows, :].astype(_F32) * pa + gb_ref[rows, :].astype(_F32) * pb
        y = jnp.dot(m.astype(_BF16), wo_ref[...], preferred_element_type=_F32)
        o_ref[rows, :] = _pick_stream(xl_ref, xc_ref, rows) + gate * y


def _merge(layer, x_lat, x_ctx, ctx_tile0, mod_l, ya, yac, yb, ybc, ga, gb, w_pa, w_pb, w_o, n_tiles):
    tile = lambda w: pl.BlockSpec((TM, w), lambda i: (i, 0))
    return pl.pallas_call(
        _merge_kernel,
        grid=(n_tiles,),
        in_specs=[*_split_specs(D_MODEL, ctx_tile0), _mod_spec(),
                  *_split_specs(Q_W, 0), *_split_specs(Q_W, 0), tile(D_MODEL), tile(D_MODEL),
                  _resident((Q_W, D_MODEL), layer), _resident((Q_W, D_MODEL), layer),
                  _resident((D_MODEL, D_MODEL), layer)],
        out_specs=tile(D_MODEL),
        out_shape=jax.ShapeDtypeStruct((n_tiles * TM, D_MODEL), _F32),
        compiler_params=_params(("arbitrary",)),
        name="merge",
    )(x_lat, x_ctx, mod_l, ya, yac, yb, ybc, ga, gb, w_pa, w_pb, w_o)


def _ffn_kernel(x_ref, mod_ref, g_ref, wg_ref, wu_ref, wd_ref, gf_ref, o_ref, *, final_norm):
    shift = mod_ref[:, 3 * D_MODEL:4 * D_MODEL]
    scale = mod_ref[:, 4 * D_MODEL:5 * D_MODEL]
    gate = mod_ref[:, 5 * D_MODEL:6 * D_MODEL]
    for rows in _sub_tiles(SUB_FFN):
        x = x_ref[rows, :]
        h = _norm_modulate(x, g_ref[...], shift, scale).astype(_BF16)
        a = jnp.dot(h, wg_ref[...], preferred_element_type=_F32)
        u = jnp.dot(h, wu_ref[...], preferred_element_type=_F32)
        act = ((a / (1.0 + jnp.exp(-a))) * u).astype(_BF16)
        y = x + gate * jnp.dot(act, wd_ref[...], preferred_element_type=_F32)
        if final_norm:
            ms = jnp.mean(y * y, axis=-1, keepdims=True)
            y = (y * lax.rsqrt(ms + NORM_EPS)) * gf_ref[...]
        o_ref[rows, :] = y


def _ffn(layer, x_all, mod_l, g2, w_gate, w_up, w_down, gf, n_tiles, final_norm):
    tile = pl.BlockSpec((TM, D_MODEL), lambda i: (i, 0))
    return pl.pallas_call(
        functools.partial(_ffn_kernel, final_norm=final_norm),
        grid=(n_tiles,),
        in_specs=[tile, _mod_spec(), _resident((1, D_MODEL), layer),
                  _resident((D_MODEL, D_FF), layer), _resident((D_MODEL, D_FF), layer),
                  _resident((D_FF, D_MODEL), layer), _resident((1, D_MODEL))],
        out_specs=tile,
        out_shape=jax.ShapeDtypeStruct((n_tiles * TM, D_MODEL), _F32),
        compiler_params=_params(("arbitrary",)),
        name="ffn",
    )(x_all, mod_l, g2, w_gate, w_up, w_down, gf)


def _rope_table():
    f32 = np.float32
    rows = SEQ // GRID_W
    row = np.repeat(np.arange(rows, dtype=np.float64), GRID_W)
    col = np.tile(np.arange(GRID_W, dtype=np.float64), rows)
    axis_dims = HEAD_DIM // 2
    inv = ROPE_THETA ** (-np.arange(0, axis_dims, 2, dtype=np.float64) / axis_dims)
    ang = np.stack([row[:, None] * inv, col[:, None] * inv], axis=1)
    cos, sin = np.cos(ang).astype(f32), np.sin(ang).astype(f32)
    lane = np.arange(LANES)
    axis = (lane % HEAD_DIM) // axis_dims
    upper = ((lane % axis_dims) // (axis_dims // 2)) == 1
    j = lane % (axis_dims // 2)
    cos_l = cos[:, axis, j]
    sin_l = sin[:, axis, j]
    sin_up = np.where(upper[None, :], sin_l, f32(0))
    sin_dn = np.where(upper[None, :], f32(0), -sin_l)
    lat = np.concatenate([cos_l, sin_up, sin_dn], axis=1)
    ident = np.concatenate([np.ones((TM, LANES), f32), np.zeros((TM, 2 * LANES), f32)], axis=1)
    return jnp.asarray(np.concatenate([lat, ident], axis=0).astype(f32))


def _head_norm_kernel(q_ref, k_ref, ones_ref, qmax_ref, kmax_ref):
    @pl.when(pl.program_id(0) == 0)
    def _():
        qmax_ref[...] = jnp.zeros(qmax_ref.shape, _F32)
        kmax_ref[...] = jnp.zeros(kmax_ref.shape, _F32)

    w = ones_ref.shape[0]
    for x_ref, o_ref in ((q_ref, qmax_ref), (k_ref, kmax_ref)):
        for c in range(0, x_ref.shape[1], w):
            x = x_ref[:, c:c + w].astype(_F32)
            ss = jnp.dot((x * x).astype(_BF16), ones_ref[...], preferred_element_type=_F32)
            o_ref[:, c:c + w] = jnp.maximum(o_ref[:, c:c + w],
                                            jnp.max(ss.reshape(TM // 8, 8, w), axis=0))


def _window_logit_bound(qa, ka, sink):
    head = np.arange(PAD_W) // HEAD_DIM
    head_ones = jnp.asarray(head[:, None] == head[None, :], _BF16)
    tile = lambda w: pl.BlockSpec((TM, w), lambda i: (i, 0))
    top = lambda w: pl.BlockSpec((8, w), lambda i: (0, 0))
    qmax, kmax = pl.pallas_call(
        _head_norm_kernel,
        grid=(ALL_TILES,),
        in_specs=[tile(Q_W), tile(PAD_W), _resident((PAD_W, PAD_W))],
        out_specs=[top(Q_W), top(PAD_W)],
        out_shape=[jax.ShapeDtypeStruct((8, Q_W), _F32), jax.ShapeDtypeStruct((8, PAD_W), _F32)],
        compiler_params=_params(("arbitrary",)),
        name="head_norms",
    )(qa, ka, head_ones)
    qk = jnp.sqrt(jnp.max(qmax) * jnp.max(kmax)) * 1.01
    return jnp.maximum(qk, jnp.max(jnp.abs(sink)) * LOG2E)


def kernel(x, c, ctx, c_ctx, w_ada, b_ada, norm1_g, norm2_g, w_in, q_norm_g, k_norm_g, sink_a,
           w_proj_a, w_proj_b, w_out, w_ffn_gate, w_ffn_up, w_ffn_down, final_norm_g):
    assert x.shape == (BATCH, SEQ, D_MODEL) and ctx.shape == (BATCH, CTX_LEN, D_MODEL)
    x_lat, x_ctx, ctx_tile0 = x.reshape(N_LAT, D_MODEL), ctx.reshape(N_CTX, D_MODEL), 0
    cc = jnp.concatenate([c, c_ctx[None, :], jnp.zeros((MOD_ROWS - BATCH - 1, D_MODEL), _F32)], axis=0)
    mod = _modulation(cc, w_ada, b_ada).reshape(DEPTH, MOD_ROWS, 1, 6 * D_MODEL)
    rope_tab = _rope_table()
    gf = final_norm_g.reshape(1, D_MODEL)
    w_in, w_proj_a, w_proj_b, w_out, w_ffn_gate, w_ffn_up, w_ffn_down = (
        w.astype(_BF16) for w in (w_in, w_proj_a, w_proj_b, w_out, w_ffn_gate, w_ffn_up, w_ffn_down))
    norm1_g = norm1_g.reshape(DEPTH, 1, D_MODEL)
    norm2_g = norm2_g.reshape(DEPTH, 1, D_MODEL)
    qg = jnp.tile(q_norm_g, (1, 2)).reshape(DEPTH, 1, LANES)
    kg = jnp.tile(k_norm_g, (1, 2)).reshape(DEPTH, 1, LANES)
    out = None
    for l in range(DEPTH):
        last = l == DEPTH - 1
        qa, ka, va, qb, kb, vb, ga, gb = _in_proj(
            l, x_lat, x_ctx, ctx_tile0, mod[l], norm1_g, w_in, rope_tab, qg, kg)
        ya = lax.cond(_window_logit_bound(qa, ka, sink_a[l]) <= LOGIT_BOUND,
                      functools.partial(_window_attention, bounded=True),
                      functools.partial(_window_attention, bounded=False), sink_a[l], qa, ka, va)
        score_bound = HEAD_DIM ** 0.5 * jnp.max(jnp.abs(q_norm_g[l])) * jnp.max(jnp.abs(k_norm_g[l]))
        yb = lax.cond(score_bound <= SCORE_BOUND,
                      functools.partial(_dense_attention, bounded=True),
                      functools.partial(_dense_attention, bounded=False), qb, kb, vb)
        n_tiles = LAT_TILES if last else ALL_TILES
        yac, ybc = (ya, yb) if last else _ctx_attention(sink_a[l], qa, ka, va, qb, kb, vb)
        x_mid = _merge(l, x_lat, x_ctx, ctx_tile0, mod[l], ya, yac, yb, ybc, ga, gb,
                       w_proj_a, w_proj_b, w_out, n_tiles)
        x_new = _ffn(l, x_mid, mod[l], norm2_g, w_ffn_gate, w_ffn_up, w_ffn_down, gf, n_tiles, last)
        if last:
            out = x_new
        else:
            x_lat, x_ctx, ctx_tile0 = x_new, x_new, LAT_TILES
    return out.reshape(BATCH, SEQ, D_MODEL)
```

```python
import functools

import jax
import jax.numpy as jnp
import numpy as np
from jax import lax
from jax.experimental import pallas as pl
from jax.experimental.pallas import tpu as pltpu

D_MODEL = 1024
BATCH = 4
SEQ = 4096
DEPTH = 4
CTX_LEN = 256
GRID_W = 64
HEAD_DIM = 64
N_HEADS = 8
N_KV = 2
WINDOW = 128
BLOCK = 128
D_FF = 2816
ROPE_THETA = 10000.0
NORM_EPS = 1e-6
NEG_INF = -1e30
Q_W = N_HEADS * HEAD_DIM
KV_W = N_KV * HEAD_DIM
IN_COLS = 2 * (Q_W + 2 * KV_W) + 2 * D_MODEL
LANES = 128
PAD_W = N_KV * LANES
HEADS_PER_KV = N_HEADS // N_KV
HEAD_ORDER = (0, 2, 1, 3)

N_LAT = BATCH * SEQ
N_CTX = BATCH * CTX_LEN
N_TOK = N_LAT + N_CTX
MOD_ROWS = 8
CTX_MOD_ROW = BATCH

TM = 1024
SUB_IN_PROJ, SUB_MERGE, SUB_FFN = 256, 512, 256
LAT_TILES = N_LAT // TM
ALL_TILES = N_TOK // TM
TILES_PER_SEQ = SEQ // TM
TQ_DENSE = 1024
TQ_DENSE_UNBOUNDED = 128
TQ_WINDOW = 2048
WIN_KEYS = BLOCK + 2 * WINDOW
WIN_Q_BOUNDED = 128
HN_TM = N_TOK // 4
TK_DENSE = 2048
TK_UNBOUNDED = 1024
TK_SUB = 256
LOG2E = 1.4426950408889634
SCORE_BOUND = 60.0
LOGIT_BOUND = 80.0
VMEM_LIMIT = 56 * 1024 * 1024

_F32 = jnp.float32
_BF16 = jnp.bfloat16
_NT = (((1,), (1,)), ((), ()))


def _params(sem):
    return pltpu.CompilerParams(dimension_semantics=sem, vmem_limit_bytes=VMEM_LIMIT)


def _resident(shape, layer=None):
    nd = len(shape)
    if layer is None:
        return pl.BlockSpec(shape, lambda *_: (0,) * nd, pipeline_mode=pl.Buffered(1))
    return pl.BlockSpec((None,) + tuple(shape), lambda *_: (layer,) + (0,) * nd,
                        pipeline_mode=pl.Buffered(1))


def _mod_row_of_tile(i):
    return jnp.minimum(i // TILES_PER_SEQ, CTX_MOD_ROW)


def _mod_spec():
    return pl.BlockSpec((None, 1, 6 * D_MODEL), lambda i: (_mod_row_of_tile(i), 0, 0))


def _split_specs(width, ctx_tile0):
    lat = pl.BlockSpec((TM, width), lambda i: (jnp.minimum(i, LAT_TILES - 1), 0))
    ctx = pl.BlockSpec((TM, width), lambda i: (jnp.maximum(i - LAT_TILES, 0) + ctx_tile0, 0))
    return lat, ctx


def _pick_stream(lat_ref, ctx_ref, rows):
    return jnp.where(pl.program_id(0) < LAT_TILES, lat_ref[rows, :], ctx_ref[rows, :])


def _sub_tiles(sub_m):
    return [slice(s * sub_m, (s + 1) * sub_m) for s in range(TM // sub_m)]


def _mod_kernel(c_ref, w_ref, b_ref, o_ref):
    c = c_ref[...]
    a = c / (1.0 + jnp.exp(-c))
    o_ref[...] = jnp.dot(a, w_ref[...], preferred_element_type=_F32) + b_ref[...]


def _modulation(cc, w_ada, b_ada):
    tn = 3072
    return pl.pallas_call(
        _mod_kernel,
        grid=(DEPTH, 6 * D_MODEL // tn),
        in_specs=[
            pl.BlockSpec((MOD_ROWS, D_MODEL), lambda l, j: (0, 0)),
            pl.BlockSpec((None, D_MODEL, tn), lambda l, j: (l, 0, j)),
            pl.BlockSpec((None, 1, tn), lambda l, j: (l, 0, j)),
        ],
        out_specs=pl.BlockSpec((None, MOD_ROWS, tn), lambda l, j: (l, 0, j)),
        out_shape=jax.ShapeDtypeStruct((DEPTH, MOD_ROWS, 6 * D_MODEL), _F32),
        compiler_params=_params(("arbitrary", "arbitrary")),
        name="adaln_mod",
    )(cc, w_ada, b_ada.reshape(DEPTH, 1, 6 * D_MODEL))


def _norm_modulate(x, g, shift, scale):
    ms = jnp.mean(x * x, axis=-1, keepdims=True)
    return (x * lax.rsqrt(ms + NORM_EPS)) * (g * (1.0 + scale)) + shift


def _rope(z, cos, sin_up, sin_dn):
    return z * cos + pltpu.roll(z, 16, 1) * sin_up + pltpu.roll(z, LANES - 16, 1) * sin_dn


def _head_rms(z, gain, lo):
    z2 = z * z
    s_all = jnp.sum(z2, axis=-1, keepdims=True)
    s_lo = jnp.sum(jnp.where(lo, z2, 0.0), axis=-1, keepdims=True)
    ms = jnp.where(lo, s_lo, s_all - s_lo) * (1.0 / HEAD_DIM)
    return (z * lax.rsqrt(ms + NORM_EPS)) * gain


def _pad_heads(z, lo, fill):
    sw = pltpu.roll(z, HEAD_DIM, 1)
    return jnp.concatenate([jnp.where(lo, z, fill), jnp.where(lo, sw, fill)], axis=1).astype(_BF16)


def _sigmoid(z):
    return 1.0 / (1.0 + jnp.exp(-z))


def _in_proj_kernel(xl_ref, xc_ref, mod_ref, g_ref, w_ref, rope_ref, qg_ref, kg_ref,
                    qa_ref, ka_ref, va_ref, qb_ref, kb_ref, vb_ref, ga_ref, gb_ref):
    shift = mod_ref[:, 0:D_MODEL]
    scale = mod_ref[:, D_MODEL:2 * D_MODEL]
    lo = lax.broadcasted_iota(jnp.int32, (1, LANES), 1) < HEAD_DIM
    qscale = HEAD_DIM ** -0.5 * LOG2E
    qg = qg_ref[...]
    for rows in _sub_tiles(SUB_IN_PROJ):
        x = _pick_stream(xl_ref, xc_ref, rows)
        h = _norm_modulate(x, g_ref[...], shift, scale).astype(_BF16)
        cos = rope_ref[rows, 0:LANES]
        sin_up = rope_ref[rows, LANES:2 * LANES]
        sin_dn = rope_ref[rows, 2 * LANES:3 * LANES]

        def proj(c0, width):
            return jnp.dot(h, w_ref[:, c0:c0 + width], preferred_element_type=_F32)

        c = 0
        z = proj(c, Q_W)
        for j in range(Q_W // LANES):
            zj = _rope(z[:, j * LANES:(j + 1) * LANES], cos, sin_up, sin_dn)
            qa_ref[rows, j * LANES:(j + 1) * LANES] = (zj * qscale).astype(_BF16)
        c += Q_W
        z = proj(c, 2 * KV_W)
        ka_ref[rows, :] = _pad_heads(_rope(z[:, 0:KV_W], cos, sin_up, sin_dn), lo, 0.0)
        va_ref[rows, :] = _pad_heads(z[:, KV_W:2 * KV_W], lo, 1.0)
        c += 2 * KV_W
        z = proj(c, Q_W)
        for j in range(Q_W // LANES):
            zj = _rope(_head_rms(z[:, j * LANES:(j + 1) * LANES], qg, lo), cos, sin_up, sin_dn)
            qb_ref[rows, j * LANES:(j + 1) * LANES] = (zj * qscale).astype(_BF16)
        c += Q_W
        z = proj(c, 2 * KV_W)
        zk = _head_rms(z[:, 0:KV_W], kg_ref[...], lo)
        kb_ref[rows, :] = _pad_heads(_rope(zk, cos, sin_up, sin_dn), lo, 0.0)
        vb_ref[rows, :] = _pad_heads(z[:, KV_W:2 * KV_W], lo, 1.0)
        c += 2 * KV_W
        ga_ref[rows, :] = _sigmoid(proj(c, D_MODEL)).astype(_BF16)
        c += D_MODEL
        gb_ref[rows, :] = _sigmoid(proj(c, D_MODEL)).astype(_BF16)


def _in_proj(layer, x_lat, x_ctx, ctx_tile0, mod_l, g1, w_in, rope_tab, qg, kg):
    tile = lambda w: pl.BlockSpec((TM, w), lambda i: (i, 0))
    rope_spec = pl.BlockSpec(
        (TM, 3 * LANES), lambda i: (jnp.where(i < LAT_TILES, i % TILES_PER_SEQ, TILES_PER_SEQ), 0))
    widths = [Q_W, PAD_W, PAD_W, Q_W, PAD_W, PAD_W, D_MODEL, D_MODEL]
    return pl.pallas_call(
        _in_proj_kernel,
        grid=(ALL_TILES,),
        in_specs=[*_split_specs(D_MODEL, ctx_tile0), _mod_spec(), _resident((1, D_MODEL), layer),
                  _resident((D_MODEL, IN_COLS), layer), rope_spec,
                  _resident((1, LANES), layer), _resident((1, LANES), layer)],
        out_specs=[tile(w) for w in widths],
        out_shape=[jax.ShapeDtypeStruct((N_TOK, w), _BF16) for w in widths],
        compiler_params=_params(("arbitrary",)),
        name="in_proj",
    )(x_lat, x_ctx, mod_l, g1, w_in, rope_tab, qg, kg)


def _stack_heads(q, g):
    pairs = [q[:, (2 * g + t) * LANES:(2 * g + t + 1) * LANES] for t in range(2)]
    swapped = [pltpu.roll(p.astype(_F32), HEAD_DIM, 1).astype(_BF16) for p in pairs]
    return jnp.concatenate(pairs + swapped, axis=0)


def _store_heads(o_ref, g, o, extra_den=None):
    tq = o.shape[0] // HEADS_PER_KV
    lo = lax.broadcasted_iota(jnp.int32, (1, LANES), 1) < HEAD_DIM
    for t in range(2):
        a = slice(t * tq, (t + 1) * tq)
        b = slice((2 + t) * tq, (3 + t) * tq)
        same = jnp.where(lo, o[a], o[b])
        cross = pltpu.roll(jnp.where(lo, o[b], o[a]), HEAD_DIM, 1)
        if extra_den is None:
            pair = jnp.where(lo, same / cross, cross / same)
        else:
            ea, eb = extra_den[a], extra_den[b]
            pair = jnp.where(lo, same / (cross + ea), cross / (same + eb))
        o_ref[:, (2 * g + t) * LANES:(2 * g + t + 1) * LANES] = pair.astype(_BF16)


def _sink_column(sink_ref, g, tq):
    blk = lax.broadcasted_iota(jnp.int32, (HEADS_PER_KV * tq, 1), 0) // tq
    col = jnp.full((HEADS_PER_KV * tq, 1), sink_ref[HEADS_PER_KV * g + HEAD_ORDER[0]], _F32)
    for i in range(1, HEADS_PER_KV):
        col = jnp.where(blk == i, sink_ref[HEADS_PER_KV * g + HEAD_ORDER[i]], col)
    return col * LOG2E


def _fold_lanes(x, op):
    out = x[:, 0:LANES]
    for c in range(1, x.shape[1] // LANES):
        out = op(out, x[:, c * LANES:(c + 1) * LANES])
    return out


def _softmax_pv(scores, values, sink_col, bounded=False):
    if bounded:
        parts = [jnp.exp2(s).astype(_BF16) for s in scores]
        sink_term = None if sink_col is None else jnp.exp2(sink_col)
    else:
        s = scores[0] if len(scores) == 1 else jnp.concatenate(scores, axis=1)
        m = jnp.max(_fold_lanes(s, jnp.maximum), axis=-1, keepdims=True)
        if sink_col is not None:
            m = jnp.maximum(m, sink_col)
        pb = jnp.exp2(s - m).astype(_BF16)
        parts, c0 = [], 0
        for v in values:
            parts.append(pb[:, c0:c0 + v.shape[0]])
            c0 += v.shape[0]
        sink_term = None if sink_col is None else jnp.exp2(sink_col - m)
    out = None
    for p, v in zip(parts, values):
        o = jnp.dot(p, v, preferred_element_type=_F32)
        out = o if out is None else out + o
    return out, sink_term


def _window_bounded_kernel(sink_ref, q_ref, k_ref, v_ref, kc_ref, vc_ref, o_ref):
    j = pl.program_id(1)
    wq, wk = WIN_Q_BOUNDED, WIN_Q_BOUNDED + 2 * WINDOW
    r = lax.broadcasted_iota(jnp.int32, (wq, wk), 0)
    col = lax.broadcasted_iota(jnp.int32, (wq, wk), 1)
    q = q_ref[...]
    for g in range(N_KV):
        cs = slice(g * LANES, (g + 1) * LANES)
        qs = _stack_heads(q, g)
        p_c = jnp.exp2(lax.dot_general(qs, kc_ref[:, cs], _NT, preferred_element_type=_F32))
        o_c = jnp.dot(p_c.astype(_BF16), vc_ref[:, cs], preferred_element_type=_F32)
        sink_term = jnp.exp2(_sink_column(sink_ref, g, wq))
        for blk in range(TQ_WINDOW // wq):
            q0 = j * TQ_WINDOW + blk * wq
            start = pl.multiple_of(jnp.clip(q0 - WINDOW, 0, SEQ - wk), BLOCK)
            bias = jnp.where(jnp.abs((start + col) - (q0 + r)) <= WINDOW, 0.0, NEG_INF)
            rows = [slice(i * TQ_WINDOW + blk * wq, i * TQ_WINDOW + (blk + 1) * wq)
                    for i in range(HEADS_PER_KV)]
            s_w = lax.dot_general(jnp.concatenate([qs[rw] for rw in rows], axis=0),
                                  k_ref[pl.ds(start, wk), cs], _NT, preferred_element_type=_F32)
            p_w = jnp.concatenate(
                [jnp.exp2(s_w[i * wq:(i + 1) * wq] + bias) for i in range(HEADS_PER_KV)], axis=0)
            o = (jnp.dot(p_w.astype(_BF16), v_ref[pl.ds(start, wk), cs], preferred_element_type=_F32)
                 + jnp.concatenate([o_c[rw] for rw in rows], axis=0))
            _store_heads(o_ref.at[blk * wq:(blk + 1) * wq, :], g, o, sink_term)


def _window_kernel(sink_ref, q_ref, k_ref, v_ref, kc_ref, vc_ref, o_ref, *, bounded):
    j = pl.program_id(1)
    r = lax.broadcasted_iota(jnp.int32, (BLOCK, WIN_KEYS), 0)
    col = lax.broadcasted_iota(jnp.int32, (BLOCK, WIN_KEYS), 1)
    for blk in range(TQ_WINDOW // BLOCK):
        q0 = j * TQ_WINDOW + blk * BLOCK
        start = pl.multiple_of(jnp.clip(q0 - WINDOW, 0, SEQ - WIN_KEYS), BLOCK)
        bias = jnp.where(jnp.abs((start + col) - (q0 + r)) <= WINDOW, 0.0, NEG_INF)
        rows = slice(blk * BLOCK, (blk + 1) * BLOCK)
        q = q_ref[rows, :]
        for g in range(N_KV):
            qs = _stack_heads(q, g)
            cs = slice(g * LANES, (g + 1) * LANES)
            kw = k_ref[pl.ds(start, WIN_KEYS), cs]
            vw = v_ref[pl.ds(start, WIN_KEYS), cs]
            s_w = lax.dot_general(qs, kw, _NT, preferred_element_type=_F32)
            s_w = jnp.concatenate(
                [s_w[i * BLOCK:(i + 1) * BLOCK] + bias for i in range(HEADS_PER_KV)], axis=0)
            s_c = lax.dot_general(qs, kc_ref[:, cs], _NT, preferred_element_type=_F32)
            o, sink_term = _softmax_pv([s_w, s_c], [vw, vc_ref[:, cs]],
                                       _sink_column(sink_ref, g, BLOCK), bounded)
            _store_heads(o_ref.at[rows, :], g, o, sink_term)


def _window_attention(sink, qa, ka, va, bounded):
    nq = SEQ // TQ_WINDOW
    lat_kv = pl.BlockSpec((SEQ, PAD_W), lambda b, j: (b, 0))
    ctx_kv = pl.BlockSpec((CTX_LEN, PAD_W), lambda b, j: (N_LAT // CTX_LEN + b, 0))
    return pl.pallas_call(
        _window_bounded_kernel if bounded else functools.partial(_window_kernel, bounded=False),
        grid=(BATCH, nq),
        in_specs=[pl.BlockSpec(memory_space=pltpu.SMEM),
                  pl.BlockSpec((TQ_WINDOW, Q_W), lambda b, j: (b * nq + j, 0)),
                  lat_kv, lat_kv, ctx_kv, ctx_kv],
        out_specs=pl.BlockSpec((TQ_WINDOW, Q_W), lambda b, j: (b * nq + j, 0)),
        out_shape=jax.ShapeDtypeStruct((N_LAT, Q_W), _BF16),
        compiler_params=_params(("arbitrary", "arbitrary")),
        name="window_attn_bounded" if bounded else "window_attn",
    )(sink, qa, ka, va, ka, va)


def _dense_kernel(q_ref, k_ref, v_ref, kc_ref, vc_ref, o_ref, acc_ref, m_ref):
    qs = _stack_heads(q_ref[...], 0)
    m_ref[...] = jnp.full(m_ref.shape, -jnp.inf, _F32)
    acc_ref[...] = jnp.zeros(acc_ref.shape, _F32)

    def update(k, v):
        s = lax.dot_general(qs, k, _NT, preferred_element_type=_F32)
        m_prev = m_ref[...]
        m_new = jnp.maximum(m_prev, jnp.max(s, axis=-1, keepdims=True))
        p = jnp.exp2(s - m_new[:, 0:1])
        acc_ref[...] = (jnp.exp2(m_prev - m_new) * acc_ref[...]
                        + jnp.dot(p.astype(_BF16), v, preferred_element_type=_F32))
        m_ref[...] = m_new

    def chunk(j, carry):
        start = pl.multiple_of(j * TK_UNBOUNDED, TK_UNBOUNDED)
        update(k_ref[pl.ds(start, TK_UNBOUNDED), :], v_ref[pl.ds(start, TK_UNBOUNDED), :])
        return carry

    lax.fori_loop(0, SEQ // TK_UNBOUNDED, chunk, 0)
    update(kc_ref[...], vc_ref[...])
    _store_heads(o_ref, 0, acc_ref[...])


def _dense_bounded_kernel(q_ref, k_ref, v_ref, kc_ref, vc_ref, o_ref, acc_ref):
    qs = _stack_heads(q_ref[...], 0)

    def term(k, v):
        p = jnp.exp2(lax.dot_general(qs, k, _NT, preferred_element_type=_F32))
        return jnp.dot(p.astype(_BF16), v, preferred_element_type=_F32)

    acc_ref[...] = term(kc_ref[...], vc_ref[...])

    def chunk(j, carry):
        for u in range(TK_DENSE // TK_SUB):
            start = pl.multiple_of(j * TK_DENSE + u * TK_SUB, TK_SUB)
            acc_ref[...] += term(k_ref[pl.ds(start, TK_SUB), :], v_ref[pl.ds(start, TK_SUB), :])
        return carry

    lax.fori_loop(0, SEQ // TK_DENSE, chunk, 0)
    _store_heads(o_ref, 0, acc_ref[...])


def _dense_attention(qb, kb, vb, bounded):
    tq = TQ_DENSE if bounded else TQ_DENSE_UNBOUNDED
    nq = SEQ // tq
    lat_kv = pl.BlockSpec((SEQ, LANES), lambda b, g, i: (b, g))
    ctx_kv = pl.BlockSpec((CTX_LEN, LANES), lambda b, g, i: (N_LAT // CTX_LEN + b, g))
    stat = pltpu.VMEM((HEADS_PER_KV * tq, LANES), _F32)
    return pl.pallas_call(
        _dense_bounded_kernel if bounded else _dense_kernel,
        grid=(BATCH, N_KV, nq),
        in_specs=[pl.BlockSpec((tq, 2 * LANES), lambda b, g, i: (b * nq + i, g)),
                  lat_kv, lat_kv, ctx_kv, ctx_kv],
        out_specs=pl.BlockSpec((tq, 2 * LANES), lambda b, g, i: (b * nq + i, g)),
        out_shape=jax.ShapeDtypeStruct((N_LAT, Q_W), _BF16),
        scratch_shapes=[stat] if bounded else [stat, stat],
        compiler_params=_params(("arbitrary", "arbitrary", "arbitrary")),
        name="dense_attn_bounded" if bounded else "dense_attn",
    )(qb, kb, vb, kb, vb)


def _ctx_kernel(sink_ref, qa_ref, ka_ref, va_ref, qb_ref, kb_ref, vb_ref, oa_ref, ob_ref):
    for q_ref, k_ref, v_ref, o_ref, use_sink in ((qa_ref, ka_ref, va_ref, oa_ref, True),
                                                 (qb_ref, kb_ref, vb_ref, ob_ref, False)):
        for blk in range(CTX_LEN // BLOCK):
            rows = slice(blk * BLOCK, (blk + 1) * BLOCK)
            q = q_ref[rows, :]
            for g in range(N_KV):
                cs = slice(g * LANES, (g + 1) * LANES)
                s = lax.dot_general(_stack_heads(q, g), k_ref[:, cs], _NT,
                                    preferred_element_type=_F32)
                sink_col = _sink_column(sink_ref, g, BLOCK) if use_sink else None
                _store_heads(o_ref.at[rows, :], g, *_softmax_pv([s], [v_ref[:, cs]], sink_col))


def _ctx_attention(sink, qa, ka, va, qb, kb, vb):
    row = lambda w: pl.BlockSpec((CTX_LEN, w), lambda b: (N_LAT // CTX_LEN + b, 0))
    out = pl.BlockSpec((CTX_LEN, Q_W), lambda b: (b, 0))
    return pl.pallas_call(
        _ctx_kernel,
        grid=(BATCH,),
        in_specs=[pl.BlockSpec(memory_space=pltpu.SMEM),
                  row(Q_W), row(PAD_W), row(PAD_W), row(Q_W), row(PAD_W), row(PAD_W)],
        out_specs=[out, out],
        out_shape=[jax.ShapeDtypeStruct((N_CTX, Q_W), _BF16)] * 2,
        compiler_params=_params(("arbitrary",)),
        name="ctx_attn",
    )(sink, qa, ka, va, qb, kb, vb)


def _merge_kernel(xl_ref, xc_ref, mod_ref, yal_ref, yac_ref, ybl_ref, ybc_ref, ga_ref, gb_ref,
                  wa_ref, wb_ref, wo_ref, o_ref):
    gate = mod_ref[:, 2 * D_MODEL:3 * D_MODEL]
    for rows in _sub_tiles(SUB_MERGE):
        ya = _pick_stream(yal_ref, yac_ref, rows)
        yb = _pick_stream(ybl_ref, ybc_ref, rows)
        pa = jnp.dot(ya, wa_ref[...], preferred_element_type=_F32)
        pb = jnp.dot(yb, wb_ref[...], preferred_element_type=_F32)
        m = ga_ref[rows, :].astype(_F32) * pa + gb_ref[rows, :].astype(_F32) * pb
        y = jnp.dot(m.astype(_BF16), wo_ref[...], preferred_element_type=_F32)
        o_ref[rows, :] = _pick_stream(xl_ref, xc_ref, rows) + gate * y


def _merge(layer, x_lat, x_ctx, ctx_tile0, mod_l, ya, yac, yb, ybc, ga, gb, w_pa, w_pb, w_o, n_tiles):
    tile = lambda w: pl.BlockSpec((TM, w), lambda i: (i, 0))
    return pl.pallas_call(
        _merge_kernel,
        grid=(n_tiles,),
        in_specs=[*_split_specs(D_MODEL, ctx_tile0), _mod_spec(),
                  *_split_specs(Q_W, 0), *_split_specs(Q_W, 0), tile(D_MODEL), tile(D_MODEL),
                  _resident((Q_W, D_MODEL), layer), _resident((Q_W, D_MODEL), layer),
                  _resident((D_MODEL, D_MODEL), layer)],
        out_specs=tile(D_MODEL),
        out_shape=jax.ShapeDtypeStruct((n_tiles * TM, D_MODEL), _F32),
        compiler_params=_params(("arbitrary",)),
        name="merge",
    )(x_lat, x_ctx, mod_l, ya, yac, yb, ybc, ga, gb, w_pa, w_pb, w_o)


def _ffn_kernel(x_ref, mod_ref, g_ref, wg_ref, wu_ref, wd_ref, gf_ref, o_ref, *, final_norm):
    shift = mod_ref[:, 3 * D_MODEL:4 * D_MODEL]
    scale = mod_ref[:, 4 * D_MODEL:5 * D_MODEL]
    gate = mod_ref[:, 5 * D_MODEL:6 * D_MODEL]
    for rows in _sub_tiles(SUB_FFN):
        x = x_ref[rows, :]
        h = _norm_modulate(x, g_ref[...], shift, scale).astype(_BF16)
        a = jnp.dot(h, wg_ref[...], preferred_element_type=_F32)
        u = jnp.dot(h, wu_ref[...], preferred_element_type=_F32)
        act = ((a / (1.0 + jnp.exp(-a))) * u).astype(_BF16)
        y = x + gate * jnp.dot(act, wd_ref[...], preferred_element_type=_F32)
        if final_norm:
            ms = jnp.mean(y * y, axis=-1, keepdims=True)
            y = (y * lax.rsqrt(ms + NORM_EPS)) * gf_ref[...]
        o_ref[rows, :] = y


def _ffn(layer, x_all, mod_l, g2, w_gate, w_up, w_down, gf, n_tiles, final_norm):
    tile = pl.BlockSpec((TM, D_MODEL), lambda i: (i, 0))
    return pl.pallas_call(
        functools.partial(_ffn_kernel, final_norm=final_norm),
        grid=(n_tiles,),
        in_specs=[tile, _mod_spec(), _resident((1, D_MODEL), layer),
                  _resident((D_MODEL, D_FF), layer), _resident((D_MODEL, D_FF), layer),
                  _resident((D_FF, D_MODEL), layer), _resident((1, D_MODEL))],
        out_specs=tile,
        out_shape=jax.ShapeDtypeStruct((n_tiles * TM, D_MODEL), _F32),
        compiler_params=_params(("arbitrary",)),
        name="ffn",
    )(x_all, mod_l, g2, w_gate, w_up, w_down, gf)


def _rope_table():
    f32 = np.float32
    rows = SEQ // GRID_W
    row = np.repeat(np.arange(rows, dtype=np.float64), GRID_W)
    col = np.tile(np.arange(GRID_W, dtype=np.float64), rows)
    axis_dims = HEAD_DIM // 2
    inv = ROPE_THETA ** (-np.arange(0, axis_dims, 2, dtype=np.float64) / axis_dims)
    ang = np.stack([row[:, None] * inv, col[:, None] * inv], axis=1)
    cos, sin = np.cos(ang).astype(f32), np.sin(ang).astype(f32)
    lane = np.arange(LANES)
    axis = (lane % HEAD_DIM) // axis_dims
    upper = ((lane % axis_dims) // (axis_dims // 2)) == 1
    j = lane % (axis_dims // 2)
    cos_l = cos[:, axis, j]
    sin_l = sin[:, axis, j]
    sin_up = np.where(upper[None, :], sin_l, f32(0))
    sin_dn = np.where(upper[None, :], f32(0), -sin_l)
    lat = np.concatenate([cos_l, sin_up, sin_dn], axis=1)
    ident = np.concatenate([np.ones((TM, LANES), f32), np.zeros((TM, 2 * LANES), f32)], axis=1)
    return jnp.asarray(np.concatenate([lat, ident], axis=0).astype(f32))


def _head_norm_kernel(q_ref, k_ref, ones_ref, qmax_ref, kmax_ref):
    @pl.when(pl.program_id(0) == 0)
    def _():
        qmax_ref[...] = jnp.zeros(qmax_ref.shape, _F32)
        kmax_ref[...] = jnp.zeros(kmax_ref.shape, _F32)

    w = ones_ref.shape[0]
    for x_ref, o_ref in ((q_ref, qmax_ref), (k_ref, kmax_ref)):
        for c in range(0, x_ref.shape[1], w):
            x = x_ref[:, c:c + w].astype(_F32)
            ss = jnp.dot((x * x).astype(_BF16), ones_ref[...], preferred_element_type=_F32)
            o_ref[:, c:c + w] = jnp.maximum(o_ref[:, c:c + w],
                                            jnp.max(ss.reshape(HN_TM // 8, 8, w), axis=0))


def _window_logit_bound(qa, ka, sink):
    head = np.arange(PAD_W) // HEAD_DIM
    head_ones = jnp.asarray(head[:, None] == head[None, :], _BF16)
    tile = lambda w: pl.BlockSpec((HN_TM, w), lambda i: (i, 0))
    top = lambda w: pl.BlockSpec((8, w), lambda i: (0, 0))
    qmax, kmax = pl.pallas_call(
        _head_norm_kernel,
        grid=(N_TOK // HN_TM,),
        in_specs=[tile(Q_W), tile(PAD_W), _resident((PAD_W, PAD_W))],
        out_specs=[top(Q_W), top(PAD_W)],
        out_shape=[jax.ShapeDtypeStruct((8, Q_W), _F32), jax.ShapeDtypeStruct((8, PAD_W), _F32)],
        compiler_params=_params(("arbitrary",)),
        name="head_norms",
    )(qa, ka, head_ones)
    qk = jnp.sqrt(jnp.max(qmax) * jnp.max(kmax)) * 1.01
    return jnp.maximum(qk, jnp.max(jnp.abs(sink)) * LOG2E)


def kernel(x, c, ctx, c_ctx, w_ada, b_ada, norm1_g, norm2_g, w_in, q_norm_g, k_norm_g, sink_a,
           w_proj_a, w_proj_b, w_out, w_ffn_gate, w_ffn_up, w_ffn_down, final_norm_g):
    assert x.shape == (BATCH, SEQ, D_MODEL) and ctx.shape == (BATCH, CTX_LEN, D_MODEL)
    x_lat, x_ctx, ctx_tile0 = x.reshape(N_LAT, D_MODEL), ctx.reshape(N_CTX, D_MODEL), 0
    cc = jnp.concatenate([c, c_ctx[None, :], jnp.zeros((MOD_ROWS - BATCH - 1, D_MODEL), _F32)], axis=0)
    mod = _modulation(cc, w_ada, b_ada).reshape(DEPTH, MOD_ROWS, 1, 6 * D_MODEL)
    rope_tab = _rope_table()
    gf = final_norm_g.reshape(1, D_MODEL)
    w_in, w_proj_a, w_proj_b, w_out, w_ffn_gate, w_ffn_up, w_ffn_down = (
        w.astype(_BF16) for w in (w_in, w_proj_a, w_proj_b, w_out, w_ffn_gate, w_ffn_up, w_ffn_down))
    norm1_g = norm1_g.reshape(DEPTH, 1, D_MODEL)
    norm2_g = norm2_g.reshape(DEPTH, 1, D_MODEL)
    qg = jnp.tile(q_norm_g, (1, 2)).reshape(DEPTH, 1, LANES)
    kg = jnp.tile(k_norm_g, (1, 2)).reshape(DEPTH, 1, LANES)
    out = None
    for l in range(DEPTH):
        last = l == DEPTH - 1
        qa, ka, va, qb, kb, vb, ga, gb = _in_proj(
            l, x_lat, x_ctx, ctx_tile0, mod[l], norm1_g, w_in, rope_tab, qg, kg)
        ya = lax.cond(_window_logit_bound(qa, ka, sink_a[l]) <= LOGIT_BOUND,
                      functools.partial(_window_attention, bounded=True),
                      functools.partial(_window_attention, bounded=False), sink_a[l], qa, ka, va)
        score_bound = HEAD_DIM ** 0.5 * jnp.max(jnp.abs(q_norm_g[l])) * jnp.max(jnp.abs(k_norm_g[l]))
        yb = lax.cond(score_bound <= SCORE_BOUND,
                      functools.partial(_dense_attention, bounded=True),
                      functools.partial(_dense_attention, bounded=False), qb, kb, vb)
        n_tiles = LAT_TILES if last else ALL_TILES
        yac, ybc = (ya, yb) if last else _ctx_attention(sink_a[l], qa, ka, va, qb, kb, vb)
        x_mid = _merge(l, x_lat, x_ctx, ctx_tile0, mod[l], ya, yac, yb, ybc, ga, gb,
                       w_proj_a, w_proj_b, w_out, n_tiles)
        x_new = _ffn(l, x_mid, mod[l], norm2_g, w_ffn_gate, w_ffn_up, w_ffn_down, gf, n_tiles, last)
        if last:
            out = x_new
        else:
            x_lat, x_ctx, ctx_tile0 = x_new, x_new, LAT_TILES
    return out.reshape(BATCH, SEQ, D_MODEL)
```

```python
import functools

import jax
import jax.numpy as jnp
import numpy as np
from jax import lax
from jax.experimental import pallas as pl
from jax.experimental.pallas import tpu as pltpu

D_MODEL = 1024
BATCH = 4
SEQ = 4096
DEPTH = 4
CTX_LEN = 256
GRID_W = 64
HEAD_DIM = 64
N_HEADS = 8
N_KV = 2
WINDOW = 128
BLOCK = 128
D_FF = 2816
ROPE_THETA = 10000.0
NORM_EPS = 1e-6
NEG_INF = -1e30
Q_W = N_HEADS * HEAD_DIM
KV_W = N_KV * HEAD_DIM
IN_COLS = 2 * (Q_W + 2 * KV_W) + 2 * D_MODEL
LANES = 128
PAD_W = N_KV * LANES
HEADS_PER_KV = N_HEADS // N_KV
HEAD_ORDER = (0, 2, 1, 3)

N_LAT = BATCH * SEQ
N_CTX = BATCH * CTX_LEN
N_TOK = N_LAT + N_CTX
MOD_ROWS = 8
CTX_MOD_ROW = BATCH

TM = 1024
SUB_IN_PROJ, SUB_MERGE, SUB_FFN = 256, 512, 256
LAT_TILES = N_LAT // TM
ALL_TILES = N_TOK // TM
TILES_PER_SEQ = SEQ // TM
TQ_DENSE = 1024
TQ_DENSE_UNBOUNDED = 128
TQ_WINDOW = 2048
WIN_KEYS = BLOCK + 2 * WINDOW
WIN_Q_BOUNDED = 128
HN_TM = N_TOK // 4
TK_DENSE = 2048
TK_UNBOUNDED = 1024
TK_SUB = 256
LOG2E = 1.4426950408889634
SCORE_BOUND = 60.0
LOGIT_BOUND = 80.0
VMEM_LIMIT = 56 * 1024 * 1024

_F32 = jnp.float32
_BF16 = jnp.bfloat16
_NT = (((1,), (1,)), ((), ()))


def _params(sem):
    return pltpu.CompilerParams(dimension_semantics=sem, vmem_limit_bytes=VMEM_LIMIT)


def _resident(shape, layer=None):
    nd = len(shape)
    if layer is None:
        return pl.BlockSpec(shape, lambda *_: (0,) * nd, pipeline_mode=pl.Buffered(1))
    return pl.BlockSpec((None,) + tuple(shape), lambda *_: (layer,) + (0,) * nd,
                        pipeline_mode=pl.Buffered(1))


def _mod_row_of_tile(i):
    return jnp.minimum(i // TILES_PER_SEQ, CTX_MOD_ROW)


def _mod_spec():
    return pl.BlockSpec((None, 1, 6 * D_MODEL), lambda i: (_mod_row_of_tile(i), 0, 0))


def _split_specs(width, ctx_tile0):
    lat = pl.BlockSpec((TM, width), lambda i: (jnp.minimum(i, LAT_TILES - 1), 0))
    ctx = pl.BlockSpec((TM, width), lambda i: (jnp.maximum(i - LAT_TILES, 0) + ctx_tile0, 0))
    return lat, ctx


def _pick_stream(lat_ref, ctx_ref, rows):
    return jnp.where(pl.program_id(0) < LAT_TILES, lat_ref[rows, :], ctx_ref[rows, :])


def _sub_tiles(sub_m):
    return [slice(s * sub_m, (s + 1) * sub_m) for s in range(TM // sub_m)]


def _mod_kernel(c_ref, w_ref, b_ref, o_ref):
    c = c_ref[...]
    a = c / (1.0 + jnp.exp(-c))
    o_ref[...] = jnp.dot(a, w_ref[...], preferred_element_type=_F32) + b_ref[...]


def _modulation(cc, w_ada, b_ada):
    tn = 3072
    return pl.pallas_call(
        _mod_kernel,
        grid=(DEPTH, 6 * D_MODEL // tn),
        in_specs=[
            pl.BlockSpec((MOD_ROWS, D_MODEL), lambda l, j: (0, 0)),
            pl.BlockSpec((None, D_MODEL, tn), lambda l, j: (l, 0, j)),
            pl.BlockSpec((None, 1, tn), lambda l, j: (l, 0, j)),
        ],
        out_specs=pl.BlockSpec((None, MOD_ROWS, tn), lambda l, j: (l, 0, j)),
        out_shape=jax.ShapeDtypeStruct((DEPTH, MOD_ROWS, 6 * D_MODEL), _F32),
        compiler_params=_params(("arbitrary", "arbitrary")),
        name="adaln_mod",
    )(cc, w_ada, b_ada.reshape(DEPTH, 1, 6 * D_MODEL))


def _norm_modulate(x, g, shift, scale):
    ms = jnp.mean(x * x, axis=-1, keepdims=True)
    return (x * lax.rsqrt(ms + NORM_EPS)) * (g * (1.0 + scale)) + shift


def _rope(z, cos, sin_up, sin_dn):
    return z * cos + pltpu.roll(z, 16, 1) * sin_up + pltpu.roll(z, LANES - 16, 1) * sin_dn


def _head_rms(z, gain, lo):
    z2 = z * z
    s_all = jnp.sum(z2, axis=-1, keepdims=True)
    s_lo = jnp.sum(jnp.where(lo, z2, 0.0), axis=-1, keepdims=True)
    ms = jnp.where(lo, s_lo, s_all - s_lo) * (1.0 / HEAD_DIM)
    return (z * lax.rsqrt(ms + NORM_EPS)) * gain


def _pad_heads(z, lo, fill):
    sw = pltpu.roll(z, HEAD_DIM, 1)
    return jnp.concatenate([jnp.where(lo, z, fill), jnp.where(lo, sw, fill)], axis=1).astype(_BF16)


def _sigmoid(z):
    return 1.0 / (1.0 + jnp.exp(-z))


def _in_proj_kernel(xl_ref, xc_ref, mod_ref, g_ref, w_ref, rope_ref, qg_ref, kg_ref,
                    qa_ref, ka_ref, va_ref, qb_ref, kb_ref, vb_ref, ga_ref, gb_ref):
    shift = mod_ref[:, 0:D_MODEL]
    scale = mod_ref[:, D_MODEL:2 * D_MODEL]
    lo = lax.broadcasted_iota(jnp.int32, (1, LANES), 1) < HEAD_DIM
    qscale = HEAD_DIM ** -0.5 * LOG2E
    qg = qg_ref[...]
    for rows in _sub_tiles(SUB_IN_PROJ):
        x = _pick_stream(xl_ref, xc_ref, rows)
        h = _norm_modulate(x, g_ref[...], shift, scale).astype(_BF16)
        cos = rope_ref[rows, 0:LANES]
        sin_up = rope_ref[rows, LANES:2 * LANES]
        sin_dn = rope_ref[rows, 2 * LANES:3 * LANES]

        def proj(c0, width):
            return jnp.dot(h, w_ref[:, c0:c0 + width], preferred_element_type=_F32)

        c = 0
        z = proj(c, Q_W)
        for j in range(Q_W // LANES):
            zj = _rope(z[:, j * LANES:(j + 1) * LANES], cos, sin_up, sin_dn)
            qa_ref[rows, j * LANES:(j + 1) * LANES] = (zj * qscale).astype(_BF16)
        c += Q_W
        z = proj(c, 2 * KV_W)
        ka_ref[rows, :] = _pad_heads(_rope(z[:, 0:KV_W], cos, sin_up, sin_dn), lo, 0.0)
        va_ref[rows, :] = _pad_heads(z[:, KV_W:2 * KV_W], lo, 1.0)
        c += 2 * KV_W
        z = proj(c, Q_W)
        for j in range(Q_W // LANES):
            zj = _rope(_head_rms(z[:, j * LANES:(j + 1) * LANES], qg, lo), cos, sin_up, sin_dn)
            qb_ref[rows, j * LANES:(j + 1) * LANES] = (zj * qscale).astype(_BF16)
        c += Q_W
        z = proj(c, 2 * KV_W)
        zk = _head_rms(z[:, 0:KV_W], kg_ref[...], lo)
        kb_ref[rows, :] = _pad_heads(_rope(zk, cos, sin_up, sin_dn), lo, 0.0)
        vb_ref[rows, :] = _pad_heads(z[:, KV_W:2 * KV_W], lo, 1.0)
        c += 2 * KV_W
        ga_ref[rows, :] = _sigmoid(proj(c, D_MODEL)).astype(_BF16)
        c += D_MODEL
        gb_ref[rows, :] = _sigmoid(proj(c, D_MODEL)).astype(_BF16)


def _in_proj(layer, x_lat, x_ctx, ctx_tile0, mod_l, g1, w_in, rope_tab, qg, kg):
    tile = lambda w: pl.BlockSpec((TM, w), lambda i: (i, 0))
    rope_spec = pl.BlockSpec(
        (TM, 3 * LANES), lambda i: (jnp.where(i < LAT_TILES, i % TILES_PER_SEQ, TILES_PER_SEQ), 0))
    widths = [Q_W, PAD_W, PAD_W, Q_W, PAD_W, PAD_W, D_MODEL, D_MODEL]
    return pl.pallas_call(
        _in_proj_kernel,
        grid=(ALL_TILES,),
        in_specs=[*_split_specs(D_MODEL, ctx_tile0), _mod_spec(), _resident((1, D_MODEL), layer),
                  _resident((D_MODEL, IN_COLS), layer), rope_spec,
                  _resident((1, LANES), layer), _resident((1, LANES), layer)],
        out_specs=[tile(w) for w in widths],
        out_shape=[jax.ShapeDtypeStruct((N_TOK, w), _BF16) for w in widths],
        compiler_params=_params(("arbitrary",)),
        name="in_proj",
    )(x_lat, x_ctx, mod_l, g1, w_in, rope_tab, qg, kg)


def _stack_heads(q, g):
    pairs = [q[:, (2 * g + t) * LANES:(2 * g + t + 1) * LANES] for t in range(2)]
    swapped = [pltpu.roll(p.astype(_F32), HEAD_DIM, 1).astype(_BF16) for p in pairs]
    return jnp.concatenate(pairs + swapped, axis=0)


def _store_heads(o_ref, g, o, extra_den=None):
    tq = o.shape[0] // HEADS_PER_KV
    lo = lax.broadcasted_iota(jnp.int32, (1, LANES), 1) < HEAD_DIM
    for t in range(2):
        a = slice(t * tq, (t + 1) * tq)
        b = slice((2 + t) * tq, (3 + t) * tq)
        same = jnp.where(lo, o[a], o[b])
        cross = pltpu.roll(jnp.where(lo, o[b], o[a]), HEAD_DIM, 1)
        if extra_den is None:
            pair = jnp.where(lo, same / cross, cross / same)
        else:
            ea, eb = extra_den[a], extra_den[b]
            pair = jnp.where(lo, same / (cross + ea), cross / (same + eb))
        o_ref[:, (2 * g + t) * LANES:(2 * g + t + 1) * LANES] = pair.astype(_BF16)


def _sink_column(sink_ref, g, tq):
    blk = lax.broadcasted_iota(jnp.int32, (HEADS_PER_KV * tq, 1), 0) // tq
    col = jnp.full((HEADS_PER_KV * tq, 1), sink_ref[HEADS_PER_KV * g + HEAD_ORDER[0]], _F32)
    for i in range(1, HEADS_PER_KV):
        col = jnp.where(blk == i, sink_ref[HEADS_PER_KV * g + HEAD_ORDER[i]], col)
    return col * LOG2E


def _fold_lanes(x, op):
    out = x[:, 0:LANES]
    for c in range(1, x.shape[1] // LANES):
        out = op(out, x[:, c * LANES:(c + 1) * LANES])
    return out


def _softmax_pv(scores, values, sink_col):
    s = scores[0] if len(scores) == 1 else jnp.concatenate(scores, axis=1)
    m = jnp.max(_fold_lanes(s, jnp.maximum), axis=-1, keepdims=True)
    if sink_col is not None:
        m = jnp.maximum(m, sink_col)
    pb = jnp.exp2(s - m).astype(_BF16)
    out, c0 = None, 0
    for v in values:
        o = jnp.dot(pb[:, c0:c0 + v.shape[0]], v, preferred_element_type=_F32)
        out = o if out is None else out + o
        c0 += v.shape[0]
    return out, (None if sink_col is None else jnp.exp2(sink_col - m))


def _window_bounded_kernel(sink_ref, q_ref, k_ref, v_ref, kc_ref, vc_ref, o_ref):
    j = pl.program_id(1)
    wq, wk = WIN_Q_BOUNDED, WIN_Q_BOUNDED + 2 * WINDOW
    r = lax.broadcasted_iota(jnp.int32, (wq, wk), 0)
    col = lax.broadcasted_iota(jnp.int32, (wq, wk), 1)
    q = q_ref[...]
    for g in range(N_KV):
        cs = slice(g * LANES, (g + 1) * LANES)
        qs = _stack_heads(q, g)
        p_c = jnp.exp2(lax.dot_general(qs, kc_ref[:, cs], _NT, preferred_element_type=_F32))
        o_c = jnp.dot(p_c.astype(_BF16), vc_ref[:, cs], preferred_element_type=_F32)
        sink_term = jnp.exp2(_sink_column(sink_ref, g, wq))
        for blk in range(TQ_WINDOW // wq):
            q0 = j * TQ_WINDOW + blk * wq
            start = pl.multiple_of(jnp.clip(q0 - WINDOW, 0, SEQ - wk), BLOCK)
            bias = jnp.where(jnp.abs((start + col) - (q0 + r)) <= WINDOW, 0.0, NEG_INF)
            rows = [slice(i * TQ_WINDOW + blk * wq, i * TQ_WINDOW + (blk + 1) * wq)
                    for i in range(HEADS_PER_KV)]
            s_w = lax.dot_general(jnp.concatenate([qs[rw] for rw in rows], axis=0),
                                  k_ref[pl.ds(start, wk), cs], _NT, preferred_element_type=_F32)
            p_w = jnp.concatenate(
                [jnp.exp2(s_w[i * wq:(i + 1) * wq] + bias) for i in range(HEADS_PER_KV)], axis=0)
            o = (jnp.dot(p_w.astype(_BF16), v_ref[pl.ds(start, wk), cs], preferred_element_type=_F32)
                 + jnp.concatenate([o_c[rw] for rw in rows], axis=0))
            _store_heads(o_ref.at[blk * wq:(blk + 1) * wq, :], g, o, sink_term)


def _window_kernel(sink_ref, q_ref, k_ref, v_ref, kc_ref, vc_ref, o_ref):
    j = pl.program_id(1)
    r = lax.broadcasted_iota(jnp.int32, (BLOCK, WIN_KEYS), 0)
    col = lax.broadcasted_iota(jnp.int32, (BLOCK, WIN_KEYS), 1)
    for blk in range(TQ_WINDOW // BLOCK):
        q0 = j * TQ_WINDOW + blk * BLOCK
        start = pl.multiple_of(jnp.clip(q0 - WINDOW, 0, SEQ - WIN_KEYS), BLOCK)
        bias = jnp.where(jnp.abs((start + col) - (q0 + r)) <= WINDOW, 0.0, NEG_INF)
        rows = slice(blk * BLOCK, (blk + 1) * BLOCK)
        q = q_ref[rows, :]
        for g in range(N_KV):
            qs = _stack_heads(q, g)
            cs = slice(g * LANES, (g + 1) * LANES)
            kw = k_ref[pl.ds(start, WIN_KEYS), cs]
            vw = v_ref[pl.ds(start, WIN_KEYS), cs]
            s_w = lax.dot_general(qs, kw, _NT, preferred_element_type=_F32)
            s_w = jnp.concatenate(
                [s_w[i * BLOCK:(i + 1) * BLOCK] + bias for i in range(HEADS_PER_KV)], axis=0)
            s_c = lax.dot_general(qs, kc_ref[:, cs], _NT, preferred_element_type=_F32)
            o, sink_term = _softmax_pv([s_w, s_c], [vw, vc_ref[:, cs]],
                                       _sink_column(sink_ref, g, BLOCK))
            _store_heads(o_ref.at[rows, :], g, o, sink_term)


def _window_attention(sink, qa, ka, va, bounded):
    nq = SEQ // TQ_WINDOW
    lat_kv = pl.BlockSpec((SEQ, PAD_W), lambda b, j: (b, 0))
    ctx_kv = pl.BlockSpec((CTX_LEN, PAD_W), lambda b, j: (N_LAT // CTX_LEN + b, 0))
    return pl.pallas_call(
        _window_bounded_kernel if bounded else _window_kernel,
        grid=(BATCH, nq),
        in_specs=[pl.BlockSpec(memory_space=pltpu.SMEM),
                  pl.BlockSpec((TQ_WINDOW, Q_W), lambda b, j: (b * nq + j, 0)),
                  lat_kv, lat_kv, ctx_kv, ctx_kv],
        out_specs=pl.BlockSpec((TQ_WINDOW, Q_W), lambda b, j: (b * nq + j, 0)),
        out_shape=jax.ShapeDtypeStruct((N_LAT, Q_W), _BF16),
        compiler_params=_params(("arbitrary", "arbitrary")),
        name="window_attn_bounded" if bounded else "window_attn",
    )(sink, qa, ka, va, ka, va)


def _dense_kernel(q_ref, k_ref, v_ref, kc_ref, vc_ref, o_ref, acc_ref, m_ref):
    qs = _stack_heads(q_ref[...], 0)
    m_ref[...] = jnp.full(m_ref.shape, -jnp.inf, _F32)
    acc_ref[...] = jnp.zeros(acc_ref.shape, _F32)

    def update(k, v):
        s = lax.dot_general(qs, k, _NT, preferred_element_type=_F32)
        m_prev = m_ref[...]
        m_new = jnp.maximum(m_prev, jnp.max(s, axis=-1, keepdims=True))
        p = jnp.exp2(s - m_new[:, 0:1])
        acc_ref[...] = (jnp.exp2(m_prev - m_new) * acc_ref[...]
                        + jnp.dot(p.astype(_BF16), v, preferred_element_type=_F32))
        m_ref[...] = m_new

    def chunk(j, carry):
        start = pl.multiple_of(j * TK_UNBOUNDED, TK_UNBOUNDED)
        update(k_ref[pl.ds(start, TK_UNBOUNDED), :], v_ref[pl.ds(start, TK_UNBOUNDED), :])
        return carry

    lax.fori_loop(0, SEQ // TK_UNBOUNDED, chunk, 0)
    update(kc_ref[...], vc_ref[...])
    _store_heads(o_ref, 0, acc_ref[...])


def _dense_bounded_kernel(q_ref, k_ref, v_ref, kc_ref, vc_ref, o_ref, acc_ref):
    qs = _stack_heads(q_ref[...], 0)

    def term(k, v):
        p = jnp.exp2(lax.dot_general(qs, k, _NT, preferred_element_type=_F32))
        return jnp.dot(p.astype(_BF16), v, preferred_element_type=_F32)

    acc_ref[...] = term(kc_ref[...], vc_ref[...])

    def chunk(j, carry):
        for u in range(TK_DENSE // TK_SUB):
            start = pl.multiple_of(j * TK_DENSE + u * TK_SUB, TK_SUB)
            acc_ref[...] += term(k_ref[pl.ds(start, TK_SUB), :], v_ref[pl.ds(start, TK_SUB), :])
        return carry

    lax.fori_loop(0, SEQ // TK_DENSE, chunk, 0)
    _store_heads(o_ref, 0, acc_ref[...])


def _dense_attention(qb, kb, vb, bounded):
    tq = TQ_DENSE if bounded else TQ_DENSE_UNBOUNDED
    nq = SEQ // tq
    lat_kv = pl.BlockSpec((SEQ, LANES), lambda b, g, i: (b, g))
    ctx_kv = pl.BlockSpec((CTX_LEN, LANES), lambda b, g, i: (N_LAT // CTX_LEN + b, g))
    stat = pltpu.VMEM((HEADS_PER_KV * tq, LANES), _F32)
    return pl.pallas_call(
        _dense_bounded_kernel if bounded else _dense_kernel,
        grid=(BATCH, N_KV, nq),
        in_specs=[pl.BlockSpec((tq, 2 * LANES), lambda b, g, i: (b * nq + i, g)),
                  lat_kv, lat_kv, ctx_kv, ctx_kv],
        out_specs=pl.BlockSpec((tq, 2 * LANES), lambda b, g, i: (b * nq + i, g)),
        out_shape=jax.ShapeDtypeStruct((N_LAT, Q_W), _BF16),
        scratch_shapes=[stat] if bounded else [stat, stat],
        compiler_params=_params(("arbitrary", "arbitrary", "arbitrary")),
        name="dense_attn_bounded" if bounded else "dense_attn",
    )(qb, kb, vb, kb, vb)


def _ctx_kernel(sink_ref, qa_ref, ka_ref, va_ref, qb_ref, kb_ref, vb_ref, oa_ref, ob_ref):
    for q_ref, k_ref, v_ref, o_ref, use_sink in ((qa_ref, ka_ref, va_ref, oa_ref, True),
                                                 (qb_ref, kb_ref, vb_ref, ob_ref, False)):
        for blk in range(CTX_LEN // BLOCK):
            rows = slice(blk * BLOCK, (blk + 1) * BLOCK)
            q = q_ref[rows, :]
            for g in range(N_KV):
                cs = slice(g * LANES, (g + 1) * LANES)
                s = lax.dot_general(_stack_heads(q, g), k_ref[:, cs], _NT,
                                    preferred_element_type=_F32)
                sink_col = _sink_column(sink_ref, g, BLOCK) if use_sink else None
                _store_heads(o_ref.at[rows, :], g, *_softmax_pv([s], [v_ref[:, cs]], sink_col))


def _ctx_attention(sink, qa, ka, va, qb, kb, vb):
    row = lambda w: pl.BlockSpec((CTX_LEN, w), lambda b: (N_LAT // CTX_LEN + b, 0))
    out = pl.BlockSpec((CTX_LEN, Q_W), lambda b: (b, 0))
    return pl.pallas_call(
        _ctx_kernel,
        grid=(BATCH,),
        in_specs=[pl.BlockSpec(memory_space=pltpu.SMEM),
                  row(Q_W), row(PAD_W), row(PAD_W), row(Q_W), row(PAD_W), row(PAD_W)],
        out_specs=[out, out],
        out_shape=[jax.ShapeDtypeStruct((N_CTX, Q_W), _BF16)] * 2,
        compiler_params=_params(("arbitrary",)),
        name="ctx_attn",
    )(sink, qa, ka, va, qb, kb, vb)


def _merge_kernel(xl_ref, xc_ref, mod_ref, yal_ref, yac_ref, ybl_ref, ybc_ref, ga_ref, gb_ref,
                  wa_ref, wb_ref, wo_ref, o_ref):
    gate = mod_ref[:, 2 * D_MODEL:3 * D_MODEL]
    for rows in _sub_tiles(SUB_MERGE):
        ya = _pick_stream(yal_ref, yac_ref, rows)
        yb = _pick_stream(ybl_ref, ybc_ref, rows)
        pa = jnp.dot(ya, wa_ref[...], preferred_element_type=_F32)
        pb = jnp.dot(yb, wb_ref[...], preferred_element_type=_F32)
        m = ga_ref[rows, :].astype(_F32) * pa + gb_ref[rows, :].astype(_F32) * pb
        y = jnp.dot(m.astype(_BF16), wo_ref[...], preferred_element_type=_F32)
        o_ref[rows, :] = _pick_stream(xl_ref, xc_ref, rows) + gate * y


def _merge(layer, x_lat, x_ctx, ctx_tile0, mod_l, ya, yac, yb, ybc, ga, gb, w_pa, w_pb, w_o, n_tiles):
    tile = lambda w: pl.BlockSpec((TM, w), lambda i: (i, 0))
    return pl.pallas_call(
        _merge_kernel,
        grid=(n_tiles,),
        in_specs=[*_split_specs(D_MODEL, ctx_tile0), _mod_spec(),
                  *_split_specs(Q_W, 0), *_split_specs(Q_W, 0), tile(D_MODEL), tile(D_MODEL),
                  _resident((Q_W, D_MODEL), layer), _resident((Q_W, D_MODEL), layer),
                  _resident((D_MODEL, D_MODEL), layer)],
        out_specs=tile(D_MODEL),
        out_shape=jax.ShapeDtypeStruct((n_tiles * TM, D_MODEL), _F32),
        compiler_params=_params(("arbitrary",)),
        name="merge",
    )(x_lat, x_ctx, mod_l, ya, yac, yb, ybc, ga, gb, w_pa, w_pb, w_o)


def _ffn_kernel(x_ref, mod_ref, g_ref, wg_ref, wu_ref, wd_ref, gf_ref, o_ref, *, final_norm):
    shift = mod_ref[:, 3 * D_MODEL:4 * D_MODEL]
    scale = mod_ref[:, 4 * D_MODEL:5 * D_MODEL]
    gate = mod_ref[:, 5 * D_MODEL:6 * D_MODEL]
    for rows in _sub_tiles(SUB_FFN):
        x = x_ref[rows, :]
        h = _norm_modulate(x, g_ref[...], shift, scale).astype(_BF16)
        a = jnp.dot(h, wg_ref[...], preferred_element_type=_F32)
        u = jnp.dot(h, wu_ref[...], preferred_element_type=_F32)
        act = ((a / (1.0 + jnp.exp(-a))) * u).astype(_BF16)
        y = x + gate * jnp.dot(act, wd_ref[...], preferred_element_type=_F32)
        if final_norm:
            ms = jnp.mean(y * y, axis=-1, keepdims=True)
            y = (y * lax.rsqrt(ms + NORM_EPS)) * gf_ref[...]
        o_ref[rows, :] = y


def _ffn(layer, x_all, mod_l, g2, w_gate, w_up, w_down, gf, n_tiles, final_norm):
    tile = pl.BlockSpec((TM, D_MODEL), lambda i: (i, 0))
    return pl.pallas_call(
        functools.partial(_ffn_kernel, final_norm=final_norm),
        grid=(n_tiles,),
        in_specs=[tile, _mod_spec(), _resident((1, D_MODEL), layer),
                  _resident((D_MODEL, D_FF), layer), _resident((D_MODEL, D_FF), layer),
                  _resident((D_FF, D_MODEL), layer), _resident((1, D_MODEL))],
        out_specs=tile,
        out_shape=jax.ShapeDtypeStruct((n_tiles * TM, D_MODEL), _F32),
        compiler_params=_params(("arbitrary",)),
        name="ffn",
    )(x_all, mod_l, g2, w_gate, w_up, w_down, gf)


def _rope_table():
    f32 = np.float32
    rows = SEQ // GRID_W
    row = np.repeat(np.arange(rows, dtype=np.float64), GRID_W)
    col = np.tile(np.arange(GRID_W, dtype=np.float64), rows)
    axis_dims = HEAD_DIM // 2
    inv = ROPE_THETA ** (-np.arange(0, axis_dims, 2, dtype=np.float64) / axis_dims)
    ang = np.stack([row[:, None] * inv, col[:, None] * inv], axis=1)
    cos, sin = np.cos(ang).astype(f32), np.sin(ang).astype(f32)
    lane = np.arange(LANES)
    axis = (lane % HEAD_DIM) // axis_dims
    upper = ((lane % axis_dims) // (axis_dims // 2)) == 1
    j = lane % (axis_dims // 2)
    cos_l = cos[:, axis, j]
    sin_l = sin[:, axis, j]
    sin_up = np.where(upper[None, :], sin_l, f32(0))
    sin_dn = np.where(upper[None, :], f32(0), -sin_l)
    lat = np.concatenate([cos_l, sin_up, sin_dn], axis=1)
    ident = np.concatenate([np.ones((TM, LANES), f32), np.zeros((TM, 2 * LANES), f32)], axis=1)
    return jnp.asarray(np.concatenate([lat, ident], axis=0).astype(f32))


def _head_norm_kernel(q_ref, k_ref, ones_ref, qmax_ref, kmax_ref):
    @pl.when(pl.program_id(0) == 0)
    def _():
        qmax_ref[...] = jnp.zeros(qmax_ref.shape, _F32)
        kmax_ref[...] = jnp.zeros(kmax_ref.shape, _F32)

    w = ones_ref.shape[0]
    for x_ref, o_ref in ((q_ref, qmax_ref), (k_ref, kmax_ref)):
        for c in range(0, x_ref.shape[1], w):
            x = x_ref[:, c:c + w].astype(_F32)
            ss = jnp.dot((x * x).astype(_BF16), ones_ref[...], preferred_element_type=_F32)
            o_ref[:, c:c + w] = jnp.maximum(o_ref[:, c:c + w],
                                            jnp.max(ss.reshape(HN_TM // 8, 8, w), axis=0))


def _window_logit_bound(qa, ka, sink):
    head = np.arange(PAD_W) // HEAD_DIM
    head_ones = jnp.asarray(head[:, None] == head[None, :], _BF16)
    tile = lambda w: pl.BlockSpec((HN_TM, w), lambda i: (i, 0))
    top = lambda w: pl.BlockSpec((8, w), lambda i: (0, 0))
    qmax, kmax = pl.pallas_call(
        _head_norm_kernel,
        grid=(N_TOK // HN_TM,),
        in_specs=[tile(Q_W), tile(PAD_W), _resident((PAD_W, PAD_W))],
        out_specs=[top(Q_W), top(PAD_W)],
        out_shape=[jax.ShapeDtypeStruct((8, Q_W), _F32), jax.ShapeDtypeStruct((8, PAD_W), _F32)],
        compiler_params=_params(("arbitrary",)),
        name="head_norms",
    )(qa, ka, head_ones)
    qk = jnp.sqrt(jnp.max(qmax) * jnp.max(kmax)) * 1.01
    return jnp.maximum(qk, jnp.max(jnp.abs(sink)) * LOG2E)


def kernel(x, c, ctx, c_ctx, w_ada, b_ada, norm1_g, norm2_g, w_in, q_norm_g, k_norm_g, sink_a,
           w_proj_a, w_proj_b, w_out, w_ffn_gate, w_ffn_up, w_ffn_down, final_norm_g):
    assert x.shape == (BATCH, SEQ, D_MODEL) and ctx.shape == (BATCH, CTX_LEN, D_MODEL)
    x_lat, x_ctx, ctx_tile0 = x.reshape(N_LAT, D_MODEL), ctx.reshape(N_CTX, D_MODEL), 0
    cc = jnp.concatenate([c, c_ctx[None, :], jnp.zeros((MOD_ROWS - BATCH - 1, D_MODEL), _F32)], axis=0)
    mod = _modulation(cc, w_ada, b_ada).reshape(DEPTH, MOD_ROWS, 1, 6 * D_MODEL)
    rope_tab = _rope_table()
    gf = final_norm_g.reshape(1, D_MODEL)
    w_in, w_proj_a, w_proj_b, w_out, w_ffn_gate, w_ffn_up, w_ffn_down = (
        w.astype(_BF16) for w in (w_in, w_proj_a, w_proj_b, w_out, w_ffn_gate, w_ffn_up, w_ffn_down))
    norm1_g = norm1_g.reshape(DEPTH, 1, D_MODEL)
    norm2_g = norm2_g.reshape(DEPTH, 1, D_MODEL)
    qg = jnp.tile(q_norm_g, (1, 2)).reshape(DEPTH, 1, LANES)
    kg = jnp.tile(k_norm_g, (1, 2)).reshape(DEPTH, 1, LANES)
    out = None
    for l in range(DEPTH):
        last = l == DEPTH - 1
        qa, ka, va, qb, kb, vb, ga, gb = _in_proj(
            l, x_lat, x_ctx, ctx_tile0, mod[l], norm1_g, w_in, rope_tab, qg, kg)
        ya = lax.cond(_window_logit_bound(qa, ka, sink_a[l]) <= LOGIT_BOUND,
                      functools.partial(_window_attention, bounded=True),
                      functools.partial(_window_attention, bounded=False), sink_a[l], qa, ka, va)
        score_bound = HEAD_DIM ** 0.5 * jnp.max(jnp.abs(q_norm_g[l])) * jnp.max(jnp.abs(k_norm_g[l]))
        yb = lax.cond(score_bound <= SCORE_BOUND,
                      functools.partial(_dense_attention, bounded=True),
                      functools.partial(_dense_attention, bounded=False), qb, kb, vb)
        n_tiles = LAT_TILES if last else ALL_TILES
        yac, ybc = (ya, yb) if last else _ctx_attention(sink_a[l], qa, ka, va, qb, kb, vb)
        x_mid = _merge(l, x_lat, x_ctx, ctx_tile0, mod[l], ya, yac, yb, ybc, ga, gb,
                       w_proj_a, w_proj_b, w_out, n_tiles)
        x_new = _ffn(l, x_mid, mod[l], norm2_g, w_ffn_gate, w_ffn_up, w_ffn_down, gf, n_tiles, last)
        if last:
            out = x_new
        else:
            x_lat, x_ctx, ctx_tile0 = x_new, x_new, LAT_TILES
    return out.reshape(BATCH, SEQ, D_MODEL)
```

```python
import functools

import jax
import jax.numpy as jnp
import numpy as np
from jax import lax
from jax.experimental import pallas as pl
from jax.experimental.pallas import tpu as pltpu

D_MODEL = 1024
BATCH = 4
SEQ = 4096
DEPTH = 4
CTX_LEN = 256
GRID_W = 64
HEAD_DIM = 64
N_HEADS = 8
N_KV = 2
WINDOW = 128
BLOCK = 128
D_FF = 2816
ROPE_THETA = 10000.0
NORM_EPS = 1e-6
NEG_INF = -1e30
Q_W = N_HEADS * HEAD_DIM
KV_W = N_KV * HEAD_DIM
IN_COLS = 2 * (Q_W + 2 * KV_W) + 2 * D_MODEL
LANES = 128
PAD_W = N_KV * LANES
HEADS_PER_KV = N_HEADS // N_KV
HEAD_ORDER = (0, 2, 1, 3)

N_LAT = BATCH * SEQ
N_CTX = BATCH * CTX_LEN
N_TOK = N_LAT + N_CTX
MOD_ROWS = 8
CTX_MOD_ROW = BATCH

TM = 1024
SUB_IN_PROJ, SUB_MERGE, SUB_FFN = 256, 512, 256
LAT_TILES = N_LAT // TM
ALL_TILES = N_TOK // TM
TILES_PER_SEQ = SEQ // TM
TQ_DENSE = 1024
TQ_DENSE_UNBOUNDED = 128
TQ_WINDOW = 2048
WIN_KEYS = BLOCK + 2 * WINDOW
WIN_Q_BOUNDED = 128
HN_TM = N_TOK // 4
TK_DENSE = 2048
TK_UNBOUNDED = 1024
TK_SUB = 256
LOG2E = 1.4426950408889634
SCORE_BOUND = 60.0
LOGIT_BOUND = 80.0
VMEM_LIMIT = 56 * 1024 * 1024

_F32 = jnp.float32
_BF16 = jnp.bfloat16
_NT = (((1,), (1,)), ((), ()))


def _params(sem, n_operands=None, fused=()):
    fusion = None if n_operands is None else [i in fused for i in range(n_operands)]
    return pltpu.CompilerParams(dimension_semantics=sem, vmem_limit_bytes=VMEM_LIMIT,
                                allow_input_fusion=fusion)


def _resident(shape, layer=None):
    nd = len(shape)
    if layer is None:
        return pl.BlockSpec(shape, lambda *_: (0,) * nd, pipeline_mode=pl.Buffered(1))
    return pl.BlockSpec((None,) + tuple(shape), lambda *_: (layer,) + (0,) * nd,
                        pipeline_mode=pl.Buffered(1))


def _mod_row_of_tile(i):
    return jnp.minimum(i // TILES_PER_SEQ, CTX_MOD_ROW)


def _mod_spec():
    return pl.BlockSpec((None, 1, 6 * D_MODEL), lambda i: (_mod_row_of_tile(i), 0, 0))


def _split_specs(width, ctx_tile0):
    lat = pl.BlockSpec((TM, width), lambda i: (jnp.minimum(i, LAT_TILES - 1), 0))
    ctx = pl.BlockSpec((TM, width), lambda i: (jnp.maximum(i - LAT_TILES, 0) + ctx_tile0, 0))
    return lat, ctx


def _pick_stream(lat_ref, ctx_ref, rows):
    return jnp.where(pl.program_id(0) < LAT_TILES, lat_ref[rows, :], ctx_ref[rows, :])


def _sub_tiles(sub_m):
    return [slice(s * sub_m, (s + 1) * sub_m) for s in range(TM // sub_m)]


def _mod_kernel(c_ref, w_ref, b_ref, o_ref):
    c = c_ref[...]
    a = c / (1.0 + jnp.exp(-c))
    o_ref[...] = jnp.dot(a, w_ref[...], preferred_element_type=_F32) + b_ref[...]


def _modulation(cc, w_ada, b_ada):
    tn = 3072
    return pl.pallas_call(
        _mod_kernel,
        grid=(DEPTH, 6 * D_MODEL // tn),
        in_specs=[
            pl.BlockSpec((MOD_ROWS, D_MODEL), lambda l, j: (0, 0)),
            pl.BlockSpec((None, D_MODEL, tn), lambda l, j: (l, 0, j)),
            pl.BlockSpec((None, 1, tn), lambda l, j: (l, 0, j)),
        ],
        out_specs=pl.BlockSpec((None, MOD_ROWS, tn), lambda l, j: (l, 0, j)),
        out_shape=jax.ShapeDtypeStruct((DEPTH, MOD_ROWS, 6 * D_MODEL), _F32),
        compiler_params=_params(("arbitrary", "arbitrary")),
        name="adaln_mod",
    )(cc, w_ada, b_ada.reshape(DEPTH, 1, 6 * D_MODEL))


def _norm_modulate(x, g, shift, scale):
    ms = jnp.mean(x * x, axis=-1, keepdims=True)
    return (x * lax.rsqrt(ms + NORM_EPS)) * (g * (1.0 + scale)) + shift


def _rope(z, cos, sin_up, sin_dn):
    return z * cos + pltpu.roll(z, 16, 1) * sin_up + pltpu.roll(z, LANES - 16, 1) * sin_dn


def _head_rms(z, gain, lo):
    z2 = z * z
    s_all = jnp.sum(z2, axis=-1, keepdims=True)
    s_lo = jnp.sum(jnp.where(lo, z2, 0.0), axis=-1, keepdims=True)
    ms = jnp.where(lo, s_lo, s_all - s_lo) * (1.0 / HEAD_DIM)
    return (z * lax.rsqrt(ms + NORM_EPS)) * gain


def _pad_heads(z, lo, fill):
    sw = pltpu.roll(z, HEAD_DIM, 1)
    return jnp.concatenate([jnp.where(lo, z, fill), jnp.where(lo, sw, fill)], axis=1).astype(_BF16)


def _sigmoid(z):
    return 1.0 / (1.0 + jnp.exp(-z))


def _in_proj_kernel(xl_ref, xc_ref, mod_ref, g_ref, w_ref, rope_ref, qg_ref, kg_ref,
                    qa_ref, ka_ref, va_ref, qb_ref, kb_ref, vb_ref, ga_ref, gb_ref):
    shift = mod_ref[:, 0:D_MODEL]
    scale = mod_ref[:, D_MODEL:2 * D_MODEL]
    lo = lax.broadcasted_iota(jnp.int32, (1, LANES), 1) < HEAD_DIM
    qscale = HEAD_DIM ** -0.5 * LOG2E
    qg = qg_ref[...]
    for rows in _sub_tiles(SUB_IN_PROJ):
        x = _pick_stream(xl_ref, xc_ref, rows)
        h = _norm_modulate(x, g_ref[...], shift, scale).astype(_BF16)
        cos = rope_ref[rows, 0:LANES]
        sin_up = rope_ref[rows, LANES:2 * LANES]
        sin_dn = rope_ref[rows, 2 * LANES:3 * LANES]

        def proj(c0, width):
            return jnp.dot(h, w_ref[:, c0:c0 + width], preferred_element_type=_F32)

        c = 0
        z = proj(c, Q_W)
        for j in range(Q_W // LANES):
            zj = _rope(z[:, j * LANES:(j + 1) * LANES], cos, sin_up, sin_dn)
            qa_ref[rows, j * LANES:(j + 1) * LANES] = (zj * qscale).astype(_BF16)
        c += Q_W
        z = proj(c, 2 * KV_W)
        ka_ref[rows, :] = _pad_heads(_rope(z[:, 0:KV_W], cos, sin_up, sin_dn), lo, 0.0)
        va_ref[rows, :] = _pad_heads(z[:, KV_W:2 * KV_W], lo, 1.0)
        c += 2 * KV_W
        z = proj(c, Q_W)
        for j in range(Q_W // LANES):
            zj = _rope(_head_rms(z[:, j * LANES:(j + 1) * LANES], qg, lo), cos, sin_up, sin_dn)
            qb_ref[rows, j * LANES:(j + 1) * LANES] = (zj * qscale).astype(_BF16)
        c += Q_W
        z = proj(c, 2 * KV_W)
        zk = _head_rms(z[:, 0:KV_W], kg_ref[...], lo)
        kb_ref[rows, :] = _pad_heads(_rope(zk, cos, sin_up, sin_dn), lo, 0.0)
        vb_ref[rows, :] = _pad_heads(z[:, KV_W:2 * KV_W], lo, 1.0)
        c += 2 * KV_W
        ga_ref[rows, :] = _sigmoid(proj(c, D_MODEL)).astype(_BF16)
        c += D_MODEL
        gb_ref[rows, :] = _sigmoid(proj(c, D_MODEL)).astype(_BF16)


def _in_proj(layer, x_lat, x_ctx, ctx_tile0, mod_l, g1, w_in, rope_tab, qg, kg):
    tile = lambda w: pl.BlockSpec((TM, w), lambda i: (i, 0))
    rope_spec = pl.BlockSpec(
        (TM, 3 * LANES), lambda i: (jnp.where(i < LAT_TILES, i % TILES_PER_SEQ, TILES_PER_SEQ), 0))
    widths = [Q_W, PAD_W, PAD_W, Q_W, PAD_W, PAD_W, D_MODEL, D_MODEL]
    return pl.pallas_call(
        _in_proj_kernel,
        grid=(ALL_TILES,),
        in_specs=[*_split_specs(D_MODEL, ctx_tile0), _mod_spec(), _resident((1, D_MODEL), layer),
                  _resident((D_MODEL, IN_COLS), layer), rope_spec,
                  _resident((1, LANES), layer), _resident((1, LANES), layer)],
        out_specs=[tile(w) for w in widths],
        out_shape=[jax.ShapeDtypeStruct((N_TOK, w), _BF16) for w in widths],
        compiler_params=_params(("arbitrary",), 8, (4,)),
        name="in_proj",
    )(x_lat, x_ctx, mod_l, g1, w_in, rope_tab, qg, kg)


def _stack_heads(q, g):
    pairs = [q[:, (2 * g + t) * LANES:(2 * g + t + 1) * LANES] for t in range(2)]
    swapped = [pltpu.roll(p.astype(_F32), HEAD_DIM, 1).astype(_BF16) for p in pairs]
    return jnp.concatenate(pairs + swapped, axis=0)


def _store_heads(o_ref, g, o, extra_den=None):
    tq = o.shape[0] // HEADS_PER_KV
    lo = lax.broadcasted_iota(jnp.int32, (1, LANES), 1) < HEAD_DIM
    for t in range(2):
        a = slice(t * tq, (t + 1) * tq)
        b = slice((2 + t) * tq, (3 + t) * tq)
        same = jnp.where(lo, o[a], o[b])
        cross = pltpu.roll(jnp.where(lo, o[b], o[a]), HEAD_DIM, 1)
        if extra_den is None:
            pair = jnp.where(lo, same / cross, cross / same)
        else:
            ea, eb = extra_den[a], extra_den[b]
            pair = jnp.where(lo, same / (cross + ea), cross / (same + eb))
        o_ref[:, (2 * g + t) * LANES:(2 * g + t + 1) * LANES] = pair.astype(_BF16)


def _sink_column(sink_ref, g, tq):
    blk = lax.broadcasted_iota(jnp.int32, (HEADS_PER_KV * tq, 1), 0) // tq
    col = jnp.full((HEADS_PER_KV * tq, 1), sink_ref[HEADS_PER_KV * g + HEAD_ORDER[0]], _F32)
    for i in range(1, HEADS_PER_KV):
        col = jnp.where(blk == i, sink_ref[HEADS_PER_KV * g + HEAD_ORDER[i]], col)
    return col * LOG2E


def _fold_lanes(x, op):
    out = x[:, 0:LANES]
    for c in range(1, x.shape[1] // LANES):
        out = op(out, x[:, c * LANES:(c + 1) * LANES])
    return out


def _softmax_pv(scores, values, sink_col):
    s = scores[0] if len(scores) == 1 else jnp.concatenate(scores, axis=1)
    m = jnp.max(_fold_lanes(s, jnp.maximum), axis=-1, keepdims=True)
    if sink_col is not None:
        m = jnp.maximum(m, sink_col)
    pb = jnp.exp2(s - m).astype(_BF16)
    out, c0 = None, 0
    for v in values:
        o = jnp.dot(pb[:, c0:c0 + v.shape[0]], v, preferred_element_type=_F32)
        out = o if out is None else out + o
        c0 += v.shape[0]
    return out, (None if sink_col is None else jnp.exp2(sink_col - m))


def _window_bounded_kernel(sink_ref, q_ref, k_ref, v_ref, kc_ref, vc_ref, o_ref):
    j = pl.program_id(1)
    wq, wk = WIN_Q_BOUNDED, WIN_Q_BOUNDED + 2 * WINDOW
    r = lax.broadcasted_iota(jnp.int32, (wq, wk), 0)
    col = lax.broadcasted_iota(jnp.int32, (wq, wk), 1)
    q = q_ref[...]
    for g in range(N_KV):
        cs = slice(g * LANES, (g + 1) * LANES)
        qs = _stack_heads(q, g)
        p_c = jnp.exp2(lax.dot_general(qs, kc_ref[:, cs], _NT, preferred_element_type=_F32))
        o_c = jnp.dot(p_c.astype(_BF16), vc_ref[:, cs], preferred_element_type=_F32)
        sink_term = jnp.exp2(_sink_column(sink_ref, g, wq))
        for blk in range(TQ_WINDOW // wq):
            q0 = j * TQ_WINDOW + blk * wq
            start = pl.multiple_of(jnp.clip(q0 - WINDOW, 0, SEQ - wk), BLOCK)
            bias = jnp.where(jnp.abs((start + col) - (q0 + r)) <= WINDOW, 0.0, NEG_INF)
            rows = [slice(i * TQ_WINDOW + blk * wq, i * TQ_WINDOW + (blk + 1) * wq)
                    for i in range(HEADS_PER_KV)]
            s_w = lax.dot_general(jnp.concatenate([qs[rw] for rw in rows], axis=0),
                                  k_ref[pl.ds(start, wk), cs], _NT, preferred_element_type=_F32)
            p_w = jnp.concatenate(
                [jnp.exp2(s_w[i * wq:(i + 1) * wq] + bias) for i in range(HEADS_PER_KV)], axis=0)
            o = (jnp.dot(p_w.astype(_BF16), v_ref[pl.ds(start, wk), cs], preferred_element_type=_F32)
                 + jnp.concatenate([o_c[rw] for rw in rows], axis=0))
            _store_heads(o_ref.at[blk * wq:(blk + 1) * wq, :], g, o, sink_term)


def _window_kernel(sink_ref, q_ref, k_ref, v_ref, kc_ref, vc_ref, o_ref):
    j = pl.program_id(1)
    r = lax.broadcasted_iota(jnp.int32, (BLOCK, WIN_KEYS), 0)
    col = lax.broadcasted_iota(jnp.int32, (BLOCK, WIN_KEYS), 1)
    for blk in range(TQ_WINDOW // BLOCK):
        q0 = j * TQ_WINDOW + blk * BLOCK
        start = pl.multiple_of(jnp.clip(q0 - WINDOW, 0, SEQ - WIN_KEYS), BLOCK)
        bias = jnp.where(jnp.abs((start + col) - (q0 + r)) <= WINDOW, 0.0, NEG_INF)
        rows = slice(blk * BLOCK, (blk + 1) * BLOCK)
        q = q_ref[rows, :]
        for g in range(N_KV):
            qs = _stack_heads(q, g)
            cs = slice(g * LANES, (g + 1) * LANES)
            kw = k_ref[pl.ds(start, WIN_KEYS), cs]
            vw = v_ref[pl.ds(start, WIN_KEYS), cs]
            s_w = lax.dot_general(qs, kw, _NT, preferred_element_type=_F32)
            s_w = jnp.concatenate(
                [s_w[i * BLOCK:(i + 1) * BLOCK] + bias for i in range(HEADS_PER_KV)], axis=0)
            s_c = lax.dot_general(qs, kc_ref[:, cs], _NT, preferred_element_type=_F32)
            o, sink_term = _softmax_pv([s_w, s_c], [vw, vc_ref[:, cs]],
                                       _sink_column(sink_ref, g, BLOCK))
            _store_heads(o_ref.at[rows, :], g, o, sink_term)


def _window_attention(sink, qa, ka, va, bounded):
    nq = SEQ // TQ_WINDOW
    lat_kv = pl.BlockSpec((SEQ, PAD_W), lambda b, j: (b, 0))
    ctx_kv = pl.BlockSpec((CTX_LEN, PAD_W), lambda b, j: (N_LAT // CTX_LEN + b, 0))
    return pl.pallas_call(
        _window_bounded_kernel if bounded else _window_kernel,
        grid=(BATCH, nq),
        in_specs=[pl.BlockSpec(memory_space=pltpu.SMEM),
                  pl.BlockSpec((TQ_WINDOW, Q_W), lambda b, j: (b * nq + j, 0)),
                  lat_kv, lat_kv, ctx_kv, ctx_kv],
        out_specs=pl.BlockSpec((TQ_WINDOW, Q_W), lambda b, j: (b * nq + j, 0)),
        out_shape=jax.ShapeDtypeStruct((N_LAT, Q_W), _BF16),
        compiler_params=_params(("arbitrary", "arbitrary")),
        name="window_attn_bounded" if bounded else "window_attn",
    )(sink, qa, ka, va, ka, va)


def _dense_kernel(q_ref, k_ref, v_ref, kc_ref, vc_ref, o_ref, acc_ref, m_ref):
    qs = _stack_heads(q_ref[...], 0)
    m_ref[...] = jnp.full(m_ref.shape, -jnp.inf, _F32)
    acc_ref[...] = jnp.zeros(acc_ref.shape, _F32)

    def update(k, v):
        s = lax.dot_general(qs, k, _NT, preferred_element_type=_F32)
        m_prev = m_ref[...]
        m_new = jnp.maximum(m_prev, jnp.max(s, axis=-1, keepdims=True))
        p = jnp.exp2(s - m_new[:, 0:1])
        acc_ref[...] = (jnp.exp2(m_prev - m_new) * acc_ref[...]
                        + jnp.dot(p.astype(_BF16), v, preferred_element_type=_F32))
        m_ref[...] = m_new

    def chunk(j, carry):
        start = pl.multiple_of(j * TK_UNBOUNDED, TK_UNBOUNDED)
        update(k_ref[pl.ds(start, TK_UNBOUNDED), :], v_ref[pl.ds(start, TK_UNBOUNDED), :])
        return carry

    lax.fori_loop(0, SEQ // TK_UNBOUNDED, chunk, 0)
    update(kc_ref[...], vc_ref[...])
    _store_heads(o_ref, 0, acc_ref[...])


def _dense_bounded_kernel(q_ref, k_ref, v_ref, kc_ref, vc_ref, o_ref, acc_ref):
    qs = _stack_heads(q_ref[...], 0)

    def term(k, v):
        p = jnp.exp2(lax.dot_general(qs, k, _NT, preferred_element_type=_F32))
        return jnp.dot(p.astype(_BF16), v, preferred_element_type=_F32)

    acc_ref[...] = term(kc_ref[...], vc_ref[...])

    def chunk(j, carry):
        for u in range(TK_DENSE // TK_SUB):
            start = pl.multiple_of(j * TK_DENSE + u * TK_SUB, TK_SUB)
            acc_ref[...] += term(k_ref[pl.ds(start, TK_SUB), :], v_ref[pl.ds(start, TK_SUB), :])
        return carry

    lax.fori_loop(0, SEQ // TK_DENSE, chunk, 0)
    _store_heads(o_ref, 0, acc_ref[...])


def _dense_attention(qb, kb, vb, bounded):
    tq = TQ_DENSE if bounded else TQ_DENSE_UNBOUNDED
    nq = SEQ // tq
    lat_kv = pl.BlockSpec((SEQ, LANES), lambda b, g, i: (b, g))
    ctx_kv = pl.BlockSpec((CTX_LEN, LANES), lambda b, g, i: (N_LAT // CTX_LEN + b, g))
    stat = pltpu.VMEM((HEADS_PER_KV * tq, LANES), _F32)
    return pl.pallas_call(
        _dense_bounded_kernel if bounded else _dense_kernel,
        grid=(BATCH, N_KV, nq),
        in_specs=[pl.BlockSpec((tq, 2 * LANES), lambda b, g, i: (b * nq + i, g)),
                  lat_kv, lat_kv, ctx_kv, ctx_kv],
        out_specs=pl.BlockSpec((tq, 2 * LANES), lambda b, g, i: (b * nq + i, g)),
        out_shape=jax.ShapeDtypeStruct((N_LAT, Q_W), _BF16),
        scratch_shapes=[stat] if bounded else [stat, stat],
        compiler_params=_params(("arbitrary", "arbitrary", "arbitrary")),
        name="dense_attn_bounded" if bounded else "dense_attn",
    )(qb, kb, vb, kb, vb)


def _ctx_kernel(sink_ref, qa_ref, ka_ref, va_ref, qb_ref, kb_ref, vb_ref, oa_ref, ob_ref):
    for q_ref, k_ref, v_ref, o_ref, use_sink in ((qa_ref, ka_ref, va_ref, oa_ref, True),
                                                 (qb_ref, kb_ref, vb_ref, ob_ref, False)):
        for blk in range(CTX_LEN // BLOCK):
            rows = slice(blk * BLOCK, (blk + 1) * BLOCK)
            q = q_ref[rows, :]
            for g in range(N_KV):
                cs = slice(g * LANES, (g + 1) * LANES)
                s = lax.dot_general(_stack_heads(q, g), k_ref[:, cs], _NT,
                                    preferred_element_type=_F32)
                sink_col = _sink_column(sink_ref, g, BLOCK) if use_sink else None
                _store_heads(o_ref.at[rows, :], g, *_softmax_pv([s], [v_ref[:, cs]], sink_col))


def _ctx_attention(sink, qa, ka, va, qb, kb, vb):
    row = lambda w: pl.BlockSpec((CTX_LEN, w), lambda b: (N_LAT // CTX_LEN + b, 0))
    out = pl.BlockSpec((CTX_LEN, Q_W), lambda b: (b, 0))
    return pl.pallas_call(
        _ctx_kernel,
        grid=(BATCH,),
        in_specs=[pl.BlockSpec(memory_space=pltpu.SMEM),
                  row(Q_W), row(PAD_W), row(PAD_W), row(Q_W), row(PAD_W), row(PAD_W)],
        out_specs=[out, out],
        out_shape=[jax.ShapeDtypeStruct((N_CTX, Q_W), _BF16)] * 2,
        compiler_params=_params(("arbitrary",)),
        name="ctx_attn",
    )(sink, qa, ka, va, qb, kb, vb)


def _merge_kernel(xl_ref, xc_ref, mod_ref, yal_ref, yac_ref, ybl_ref, ybc_ref, ga_ref, gb_ref,
                  wa_ref, wb_ref, wo_ref, o_ref):
    gate = mod_ref[:, 2 * D_MODEL:3 * D_MODEL]
    for rows in _sub_tiles(SUB_MERGE):
        ya = _pick_stream(yal_ref, yac_ref, rows)
        yb = _pick_stream(ybl_ref, ybc_ref, rows)
        pa = jnp.dot(ya, wa_ref[...], preferred_element_type=_F32)
        pb = jnp.dot(yb, wb_ref[...], preferred_element_type=_F32)
        m = ga_ref[rows, :].astype(_F32) * pa + gb_ref[rows, :].astype(_F32) * pb
        y = jnp.dot(m.astype(_BF16), wo_ref[...], preferred_element_type=_F32)
        o_ref[rows, :] = _pick_stream(xl_ref, xc_ref, rows) + gate * y


def _merge(layer, x_lat, x_ctx, ctx_tile0, mod_l, ya, yac, yb, ybc, ga, gb, w_pa, w_pb, w_o, n_tiles):
    tile = lambda w: pl.BlockSpec((TM, w), lambda i: (i, 0))
    return pl.pallas_call(
        _merge_kernel,
        grid=(n_tiles,),
        in_specs=[*_split_specs(D_MODEL, ctx_tile0), _mod_spec(),
                  *_split_specs(Q_W, 0), *_split_specs(Q_W, 0), tile(D_MODEL), tile(D_MODEL),
                  _resident((Q_W, D_MODEL), layer), _resident((Q_W, D_MODEL), layer),
                  _resident((D_MODEL, D_MODEL), layer)],
        out_specs=tile(D_MODEL),
        out_shape=jax.ShapeDtypeStruct((n_tiles * TM, D_MODEL), _F32),
        compiler_params=_params(("arbitrary",), 12, (9, 10, 11)),
        name="merge",
    )(x_lat, x_ctx, mod_l, ya, yac, yb, ybc, ga, gb, w_pa, w_pb, w_o)


def _ffn_kernel(x_ref, mod_ref, g_ref, wg_ref, wu_ref, wd_ref, gf_ref, o_ref, *, final_norm):
    shift = mod_ref[:, 3 * D_MODEL:4 * D_MODEL]
    scale = mod_ref[:, 4 * D_MODEL:5 * D_MODEL]
    gate = mod_ref[:, 5 * D_MODEL:6 * D_MODEL]
    for rows in _sub_tiles(SUB_FFN):
        x = x_ref[rows, :]
        h = _norm_modulate(x, g_ref[...], shift, scale).astype(_BF16)
        a = jnp.dot(h, wg_ref[...], preferred_element_type=_F32)
        u = jnp.dot(h, wu_ref[...], preferred_element_type=_F32)
        act = ((a / (1.0 + jnp.exp(-a))) * u).astype(_BF16)
        y = x + gate * jnp.dot(act, wd_ref[...], preferred_element_type=_F32)
        if final_norm:
            ms = jnp.mean(y * y, axis=-1, keepdims=True)
            y = (y * lax.rsqrt(ms + NORM_EPS)) * gf_ref[...]
        o_ref[rows, :] = y


def _ffn(layer, x_all, mod_l, g2, w_gate, w_up, w_down, gf, n_tiles, final_norm):
    tile = pl.BlockSpec((TM, D_MODEL), lambda i: (i, 0))
    return pl.pallas_call(
        functools.partial(_ffn_kernel, final_norm=final_norm),
        grid=(n_tiles,),
        in_specs=[tile, _mod_spec(), _resident((1, D_MODEL), layer),
                  _resident((D_MODEL, D_FF), layer), _resident((D_MODEL, D_FF), layer),
                  _resident((D_FF, D_MODEL), layer), _resident((1, D_MODEL))],
        out_specs=tile,
        out_shape=jax.ShapeDtypeStruct((n_tiles * TM, D_MODEL), _F32),
        compiler_params=_params(("arbitrary",), 7, (3, 4, 5)),
        name="ffn",
    )(x_all, mod_l, g2, w_gate, w_up, w_down, gf)


def _rope_table():
    f32 = np.float32
    rows = SEQ // GRID_W
    row = np.repeat(np.arange(rows, dtype=np.float64), GRID_W)
    col = np.tile(np.arange(GRID_W, dtype=np.float64), rows)
    axis_dims = HEAD_DIM // 2
    inv = ROPE_THETA ** (-np.arange(0, axis_dims, 2, dtype=np.float64) / axis_dims)
    ang = np.stack([row[:, None] * inv, col[:, None] * inv], axis=1)
    cos, sin = np.cos(ang).astype(f32), np.sin(ang).astype(f32)
    lane = np.arange(LANES)
    axis = (lane % HEAD_DIM) // axis_dims
    upper = ((lane % axis_dims) // (axis_dims // 2)) == 1
    j = lane % (axis_dims // 2)
    cos_l = cos[:, axis, j]
    sin_l = sin[:, axis, j]
    sin_up = np.where(upper[None, :], sin_l, f32(0))
    sin_dn = np.where(upper[None, :], f32(0), -sin_l)
    lat = np.concatenate([cos_l, sin_up, sin_dn], axis=1)
    ident = np.concatenate([np.ones((TM, LANES), f32), np.zeros((TM, 2 * LANES), f32)], axis=1)
    return jnp.asarray(np.concatenate([lat, ident], axis=0).astype(f32))


def _head_norm_kernel(q_ref, k_ref, ones_ref, qmax_ref, kmax_ref):
    @pl.when(pl.program_id(0) == 0)
    def _():
        qmax_ref[...] = jnp.zeros(qmax_ref.shape, _F32)
        kmax_ref[...] = jnp.zeros(kmax_ref.shape, _F32)

    w = ones_ref.shape[0]
    for x_ref, o_ref in ((q_ref, qmax_ref), (k_ref, kmax_ref)):
        for c in range(0, x_ref.shape[1], w):
            x = x_ref[:, c:c + w].astype(_F32)
            ss = jnp.dot((x * x).astype(_BF16), ones_ref[...], preferred_element_type=_F32)
            o_ref[:, c:c + w] = jnp.maximum(o_ref[:, c:c + w],
                                            jnp.max(ss.reshape(HN_TM // 8, 8, w), axis=0))


def _window_logit_bound(qa, ka, sink):
    head = np.arange(PAD_W) // HEAD_DIM
    head_ones = jnp.asarray(head[:, None] == head[None, :], _BF16)
    tile = lambda w: pl.BlockSpec((HN_TM, w), lambda i: (i, 0))
    top = lambda w: pl.BlockSpec((8, w), lambda i: (0, 0))
    qmax, kmax = pl.pallas_call(
        _head_norm_kernel,
        grid=(N_TOK // HN_TM,),
        in_specs=[tile(Q_W), tile(PAD_W), _resident((PAD_W, PAD_W))],
        out_specs=[top(Q_W), top(PAD_W)],
        out_shape=[jax.ShapeDtypeStruct((8, Q_W), _F32), jax.ShapeDtypeStruct((8, PAD_W), _F32)],
        compiler_params=_params(("arbitrary",)),
        name="head_norms",
    )(qa, ka, head_ones)
    qk = jnp.sqrt(jnp.max(qmax) * jnp.max(kmax)) * 1.01
    return jnp.maximum(qk, jnp.max(jnp.abs(sink)) * LOG2E)


def kernel(x, c, ctx, c_ctx, w_ada, b_ada, norm1_g, norm2_g, w_in, q_norm_g, k_norm_g, sink_a,
           w_proj_a, w_proj_b, w_out, w_ffn_gate, w_ffn_up, w_ffn_down, final_norm_g):
    assert x.shape == (BATCH, SEQ, D_MODEL) and ctx.shape == (BATCH, CTX_LEN, D_MODEL)
    x_lat, x_ctx, ctx_tile0 = x.reshape(N_LAT, D_MODEL), ctx.reshape(N_CTX, D_MODEL), 0
    cc = jnp.concatenate([c, c_ctx[None, :], jnp.zeros((MOD_ROWS - BATCH - 1, D_MODEL), _F32)], axis=0)
    mod = _modulation(cc, w_ada, b_ada).reshape(DEPTH, MOD_ROWS, 1, 6 * D_MODEL)
    rope_tab = _rope_table()
    gf = final_norm_g.reshape(1, D_MODEL)
    w_in, w_proj_a, w_proj_b, w_out, w_ffn_gate, w_ffn_up, w_ffn_down = (
        w.astype(_BF16) for w in (w_in, w_proj_a, w_proj_b, w_out, w_ffn_gate, w_ffn_up, w_ffn_down))
    norm1_g = norm1_g.reshape(DEPTH, 1, D_MODEL)
    norm2_g = norm2_g.reshape(DEPTH, 1, D_MODEL)
    qg = jnp.tile(q_norm_g, (1, 2)).reshape(DEPTH, 1, LANES)
    kg = jnp.tile(k_norm_g, (1, 2)).reshape(DEPTH, 1, LANES)
    out = None
    for l in range(DEPTH):
        last = l == DEPTH - 1
        qa, ka, va, qb, kb, vb, ga, gb = _in_proj(
            l, x_lat, x_ctx, ctx_tile0, mod[l], norm1_g, w_in, rope_tab, qg, kg)
        ya = lax.cond(_window_logit_bound(qa, ka, sink_a[l]) <= LOGIT_BOUND,
                      functools.partial(_window_attention, bounded=True),
                      functools.partial(_window_attention, bounded=False), sink_a[l], qa, ka, va)
        score_bound = HEAD_DIM ** 0.5 * jnp.max(jnp.abs(q_norm_g[l])) * jnp.max(jnp.abs(k_norm_g[l]))
        yb = lax.cond(score_bound <= SCORE_BOUND,
                      functools.partial(_dense_attention, bounded=True),
                      functools.partial(_dense_attention, bounded=False), qb, kb, vb)
        n_tiles = LAT_TILES if last else ALL_TILES
        yac, ybc = (ya, yb) if last else _ctx_attention(sink_a[l], qa, ka, va, qb, kb, vb)
        x_mid = _merge(l, x_lat, x_ctx, ctx_tile0, mod[l], ya, yac, yb, ybc, ga, gb,
                       w_proj_a, w_proj_b, w_out, n_tiles)
        x_new = _ffn(l, x_mid, mod[l], norm2_g, w_ffn_gate, w_ffn_up, w_ffn_down, gf, n_tiles, last)
        if last:
            out = x_new
        else:
            x_lat, x_ctx, ctx_tile0 = x_new, x_new, LAT_TILES
    return out.reshape(BATCH, SEQ, D_MODEL)
```
